```python
import jax, jax.numpy as jnp
from jax import lax
import numpy as np

D_MODEL = 1024
BATCH = 4
SEQ = 8192
DEPTH = 2

CHUNK = 64
NORM_EPS = 1e-6
A_HEADS = 8
A_HEAD_DIM = 64
A_WIDTH = A_HEADS * A_HEAD_DIM
A_LEFT_CHUNKS = 8
A_BAND = (A_LEFT_CHUNKS + 1) * CHUNK
A_MAX_REL = 256
B_GROUPS = 4
B_BLOCK = 128
B_WIDTH = D_MODEL // 2
B_GROUP_DIM = B_WIDTH // B_GROUPS
AB_IN = 3 * A_WIDTH + 2 * B_WIDTH
AB_MIX = A_WIDTH + B_WIDTH
C_HEADS = 4
C_KEY_DIM = D_MODEL // 2
C_VAL_DIM = D_MODEL
C_DK = C_KEY_DIM // C_HEADS
C_DV = C_VAL_DIM // C_HEADS
C_GATE_RANK = 16
C_GATE_TAU = 16.0
C_IN = 2 * C_KEY_DIM + 2 * C_VAL_DIM + C_GATE_RANK
D_FF = ((-(-8 * D_MODEL // 3) + 255) // 256) * 256
N_EVEN = (DEPTH + 1) // 2
N_ODD = DEPTH // 2

kernel_name = "hybrid_chunk_attn_gmlp_gla"


def rms_norm(x, g):
    xf = x.astype(jnp.float32)
    y = xf * lax.rsqrt(jnp.mean(xf * xf, axis=-1, keepdims=True) + NORM_EPS)
    return (y * g.astype(jnp.float32)).astype(x.dtype)


def layer_norm(x, g, b):
    xf = x.astype(jnp.float32)
    mu = jnp.mean(xf, axis=-1, keepdims=True)
    var = jnp.mean(jnp.square(xf - mu), axis=-1, keepdims=True)
    y = (xf - mu) * lax.rsqrt(var + NORM_EPS)
    return (y * g.astype(jnp.float32) + b.astype(jnp.float32)).astype(x.dtype)


def chunk_band_attention(q, k, v, rel_bias):
    b, s, h, d = q.shape
    nc = s // CHUNK
    f32 = jnp.float32
    qc = q.reshape(b, nc, CHUNK, h, d).astype(f32) * (d ** -0.5)
    pad = ((0, 0), (A_LEFT_CHUNKS, 0), (0, 0), (0, 0), (0, 0))
    kp = jnp.pad(k.reshape(b, nc, CHUNK, h, d), pad)
    vp = jnp.pad(v.reshape(b, nc, CHUNK, h, d), pad)
    idx = jnp.arange(nc)[:, None] + jnp.arange(A_LEFT_CHUNKS + 1)[None, :]
    kb = kp[:, idx].reshape(b, nc, A_BAND, h, d).astype(f32)
    vb = vp[:, idx].reshape(b, nc, A_BAND, h, d).astype(f32)
    scores = jnp.einsum('bcqhd,bckhd->bhcqk', qc, kb)
    qi = jnp.arange(CHUNK)[:, None]
    kj = jnp.arange(A_BAND)[None, :]
    rel = jnp.clip(qi + A_LEFT_CHUNKS * CHUNK - kj, -A_MAX_REL, A_MAX_REL) + A_MAX_REL
    bias = rel_bias.astype(f32)[:, rel]
    valid = jnp.repeat((idx - A_LEFT_CHUNKS) >= 0, CHUNK, axis=1)
    scores = jnp.where(valid[None, None, :, None, :], scores + bias[None, :, None],
                       jnp.finfo(f32).min)
    p = jax.nn.softmax(scores, axis=-1)
    out = jnp.einsum('bhcqk,bckhd->bcqhd', p, vb)
    return out.reshape(b, s, h * d).astype(q.dtype)


def chunk_spatial_gating(u, v, ln_g, ln_b, w_s, b_s):
    b, s, _ = u.shape
    nb = s // B_BLOCK
    v = layer_norm(v, ln_g, ln_b)
    vg = v.reshape(b, nb, B_BLOCK, B_GROUPS, B_GROUP_DIM)
    causal = jnp.tril(jnp.ones((B_BLOCK, B_BLOCK), dtype=bool))
    w = jnp.where(causal[None], w_s, jnp.zeros_like(w_s))
    f = jnp.einsum('gts,bnsgc->bntgc', w, vg) + b_s.T[None, None, :, :, None]
    return u * f.reshape(b, s, B_WIDTH).astype(u.dtype)


def attn_gmlp_mixer(h, w_in, rel_bias, ln_g, ln_b, w_s, b_s, w_out):
    b, s, _ = h.shape
    proj = h @ w_in
    q, k, v, zu, zv = jnp.split(
        proj, [A_WIDTH, 2 * A_WIDTH, 3 * A_WIDTH, 3 * A_WIDTH + B_WIDTH], axis=-1)
    heads = lambda t: t.reshape(b, s, A_HEADS, A_HEAD_DIM)
    a_out = chunk_band_attention(heads(q), heads(k), heads(v), rel_bias)
    b_out = chunk_spatial_gating(jax.nn.gelu(zu, approximate=False),
                                 jax.nn.gelu(zv, approximate=False),
                                 ln_g, ln_b, w_s, b_s)
    return jnp.concatenate([a_out, b_out], axis=-1) @ w_out


def gla_chunk_scan(q, k, v, log_a):
    b, s, h, dk = q.shape
    dv = v.shape[-1]
    nc = s // CHUNK

    def to_chunks(t):
        return t.reshape(b, nc, CHUNK, h, t.shape[-1]).transpose(1, 0, 3, 2, 4)

    causal = jnp.tril(jnp.ones((CHUNK, CHUNK), dtype=bool))[:, :, None]

    def step(state, inp):
        qc, kc, vc, lac = inp
        cum = jnp.cumsum(lac, axis=2)
        diff = cum[:, :, :, None, :] - cum[:, :, None, :, :]
        decay = jnp.exp(jnp.where(causal, diff, -jnp.inf))
        attn = jnp.einsum('bhid,bhjd,bhijd->bhij', qc, kc, decay)
        o = (jnp.einsum('bhij,bhje->bhie', attn, vc)
             + jnp.einsum('bhid,bhde->bhie', qc * jnp.exp(cum), state))
        last = cum[:, :, -1:, :]
        state = (jnp.exp(last[:, :, 0, :])[..., None] * state
                 + jnp.einsum('bhjd,bhje->bhde', kc * jnp.exp(last - cum), vc))
        return state, o

    s0 = jnp.zeros((b, h, dk, dv), jnp.float32)
    _, o = lax.scan(step, s0, (to_chunks(q), to_chunks(k), to_chunks(v), to_chunks(log_a)))
    return o.transpose(1, 0, 3, 2, 4).reshape(b, s, h, dv)


def gla_mixer(h, w_in, w_a2, b_a, norm_g, w_out):
    b, s, _ = h.shape
    proj = h @ w_in
    q, k, v, g, a_low = jnp.split(
        proj, [C_KEY_DIM, 2 * C_KEY_DIM, 2 * C_KEY_DIM + C_VAL_DIM,
               2 * C_KEY_DIM + 2 * C_VAL_DIM], axis=-1)
    log_a = jax.nn.log_sigmoid((a_low @ w_a2 + b_a).astype(jnp.float32)) / C_GATE_TAU
    heads = lambda t, d: t.reshape(b, s, C_HEADS, d).astype(jnp.float32)
    o = gla_chunk_scan(heads(q, C_DK) * (C_DK ** -0.5), heads(k, C_DK),
                       heads(v, C_DV), heads(log_a, C_DK))
    o = rms_norm(o, norm_g.reshape(C_HEADS, C_DV))
    o = o.reshape(b, s, C_VAL_DIM).astype(h.dtype) * jax.nn.silu(g)
    return o @ w_out


def swiglu(h, w_gate, w_up, w_down):
    return (jax.nn.silu(h @ w_gate) * (h @ w_up)) @ w_down


def setup_inputs(seed: int = 0) -> dict:
    key = jax.random.key(seed)
    ks = jax.random.split(key, 24)
    f32 = jnp.float32
    nrm = lambda k, shape, scale: jax.random.normal(k, shape, f32) * scale
    return {
        "x": nrm(ks[0], (BATCH, SEQ, D_MODEL), 1.0),
        "pre_mix_g": 1.0 + nrm(ks[1], (DEPTH, D_MODEL), 0.05),
        "post_mix_g": 1.0 + nrm(ks[2], (DEPTH, D_MODEL), 0.05),
        "pre_ffn_g": 1.0 + nrm(ks[3], (DEPTH, D_MODEL), 0.05),
        "post_ffn_g": 1.0 + nrm(ks[4], (DEPTH, D_MODEL), 0.05),
        "ab_w_in": nrm(ks[5], (N_EVEN, D_MODEL, AB_IN), D_MODEL ** -0.5),
        "a_rel_bias": nrm(ks[6], (N_EVEN, A_HEADS, 2 * A_MAX_REL + 1), 0.5),
        "b_ln_g": 1.0 + nrm(ks[7], (N_EVEN, B_WIDTH), 0.05),
        "b_ln_b": nrm(ks[8], (N_EVEN, B_WIDTH), 0.05),
        "b_w_s": nrm(ks[9], (N_EVEN, B_GROUPS, B_BLOCK, B_BLOCK), B_BLOCK ** -0.5),
        "b_b_s": 1.0 + nrm(ks[10], (N_EVEN, B_GROUPS, B_BLOCK), 0.1),
        "ab_w_out": nrm(ks[11], (N_EVEN, AB_MIX, D_MODEL), AB_MIX ** -0.5),
        "c_w_in": nrm(ks[12], (N_ODD, D_MODEL, C_IN), D_MODEL ** -0.5),
        "c_w_a2": nrm(ks[13], (N_ODD, C_GATE_RANK, C_KEY_DIM), C_GATE_RANK ** -0.5),
        "c_b_a": nrm(ks[14], (N_ODD, C_KEY_DIM), 0.1),
        "c_norm_g": 1.0 + nrm(ks[15], (N_ODD, C_VAL_DIM), 0.05),
        "c_w_out": nrm(ks[16], (N_ODD, C_VAL_DIM, D_MODEL), C_VAL_DIM ** -0.5),
        "ffn_w_gate": nrm(ks[17], (DEPTH, D_MODEL, D_FF), D_MODEL ** -0.5),
        "ffn_w_up": nrm(ks[18], (DEPTH, D_MODEL, D_FF), D_MODEL ** -0.5),
        "ffn_w_down": nrm(ks[19], (DEPTH, D_FF, D_MODEL), D_FF ** -0.5),
    }


def reference(x, pre_mix_g, post_mix_g, pre_ffn_g, post_ffn_g,
              ab_w_in, a_rel_bias, b_ln_g, b_ln_b, b_w_s, b_b_s, ab_w_out,
              c_w_in, c_w_a2, c_b_a, c_norm_g, c_w_out,
              ffn_w_gate, ffn_w_up, ffn_w_down):
    for i in range(DEPTH):
        j = i // 2
        h = rms_norm(x, pre_mix_g[i])
        if i % 2 == 0:
            m = attn_gmlp_mixer(h, ab_w_in[j], a_rel_bias[j], b_ln_g[j], b_ln_b[j],
                                b_w_s[j], b_b_s[j], ab_w_out[j])
        else:
            m = gla_mixer(h, c_w_in[j], c_w_a2[j], c_b_a[j], c_norm_g[j], c_w_out[j])
        x = x + rms_norm(m, post_mix_g[i])
        h = rms_norm(x, pre_ffn_g[i])
        x = x + rms_norm(swiglu(h, ffn_w_gate[i], ffn_w_up[i], ffn_w_down[i]), post_ffn_g[i])
    return x
```

```python
import functools

import jax
import jax.numpy as jnp
from jax import lax
from jax.experimental import pallas as pl
from jax.experimental.pallas import tpu as pltpu

F32 = jnp.float32
BF16 = jnp.bfloat16

D_MODEL = 1024
CHUNK = 64
NORM_EPS = 1e-6
A_HEADS = 8
A_HEAD_DIM = 64
A_WIDTH = A_HEADS * A_HEAD_DIM
A_LEFT_CHUNKS = 8
A_MAX_REL = 256
B_GROUPS = 4
B_BLOCK = 128
B_WIDTH = D_MODEL // 2
AB_IN = 3 * A_WIDTH + 2 * B_WIDTH
C_HEADS = 4
C_KEY_DIM = D_MODEL // 2
C_VAL_DIM = D_MODEL
C_DK = C_KEY_DIM // C_HEADS
C_DV = C_VAL_DIM // C_HEADS
C_GATE_RANK = 16
C_GATE_TAU = 16.0
C_MAIN = 2 * C_KEY_DIM + 2 * C_VAL_DIM
D_FF = 2816

LANES = 128
VMEM_LIMIT_BYTES = 56 * 1024 * 1024

NEG_BIG = float(jnp.finfo(jnp.float32).min)

TM_IN0 = 512
TQ_ATTN = 256
TM_FFN = 256
TM_IN1 = 256
TC_GLA = 256
ATTN_KEYS = TQ_ATTN + A_LEFT_CHUNKS * CHUNK
BIAS_ROW = 1024


def _rms_norm(x, g):
    ms = jnp.mean(x * x, axis=-1, keepdims=True)
    return x * lax.rsqrt(ms + NORM_EPS) * g


def _gelu(x):
    return 0.5 * x * (1.0 + lax.erf(x * (2.0 ** -0.5)))


def _const_spec(shape):
    nd = len(shape)
    return pl.BlockSpec(shape, lambda *_: (0,) * nd, pipeline_mode=pl.Buffered(1))


def _params(sem):
    return pltpu.CompilerParams(dimension_semantics=sem,
                                vmem_limit_bytes=VMEM_LIMIT_BYTES)


def _l0_in_kernel(x_ref, g_ref, w_ref, lng_ref, lnb_ref, ws_ref, bst_ref,
                  q_ref, k_ref, v_ref, b_ref):
    tm = x_ref.shape[0]
    h = _rms_norm(x_ref[...], g_ref[...]).astype(BF16)
    proj = jnp.dot(h, w_ref[...], preferred_element_type=F32)
    q_ref[...] = (proj[:, 0:A_WIDTH] * (A_HEAD_DIM ** -0.5)).astype(BF16)
    k_ref[...] = proj[:, A_WIDTH:2 * A_WIDTH].astype(BF16)
    v_ref[...] = proj[:, 2 * A_WIDTH:3 * A_WIDTH].astype(BF16)
    u = _gelu(proj[:, 3 * A_WIDTH:3 * A_WIDTH + B_WIDTH])
    zv = _gelu(proj[:, 3 * A_WIDTH + B_WIDTH:AB_IN])
    mu = jnp.mean(zv, axis=-1, keepdims=True)
    d = zv - mu
    var = jnp.mean(d * d, axis=-1, keepdims=True)
    vln = (d * lax.rsqrt(var + NORM_EPS) * lng_ref[...] + lnb_ref[...]).astype(BF16)
    row = lax.broadcasted_iota(jnp.int32, (B_BLOCK, B_BLOCK), 0)
    col = lax.broadcasted_iota(jnp.int32, (B_BLOCK, B_BLOCK), 1)
    causal = col <= row
    for g in range(B_GROUPS):
        wg = jnp.where(causal, ws_ref[g], 0.0).astype(BF16)
        bcol = bst_ref[:, g:g + 1]
        cs = slice(g * LANES, (g + 1) * LANES)
        for n in range(tm // B_BLOCK):
            rs = slice(n * B_BLOCK, (n + 1) * B_BLOCK)
            f = jnp.dot(wg, vln[rs, cs], preferred_element_type=F32) + bcol
            b_ref[rs, cs] = (u[rs, cs] * f).astype(BF16)


def _l0_in(x2d, g, w_in, ln_g, ln_b, w_s, b_s_t):
    t = x2d.shape[0]
    tm = TM_IN0
    tok = lambda n: pl.BlockSpec((tm, n), lambda i: (i, 0))
    out = jax.ShapeDtypeStruct((t, A_WIDTH), BF16)
    return pl.pallas_call(
        _l0_in_kernel,
        grid=(t // tm,),
        in_specs=[tok(D_MODEL), _const_spec((1, D_MODEL)), _const_spec((D_MODEL, AB_IN)),
                  _const_spec((1, B_WIDTH)), _const_spec((1, B_WIDTH)),
                  _const_spec((B_GROUPS, B_BLOCK, B_BLOCK)), _const_spec((B_BLOCK, B_GROUPS))],
        out_specs=[tok(A_WIDTH)] * 4,
        out_shape=[out] * 4,
        compiler_params=_params(("parallel",)),
        name="l0_in_proj_gmlp",
    )(x2d, g, w_in, ln_g, ln_b, w_s, b_s_t)


def _attn_kernel(q_ref, k0_ref, k1_ref, k2_ref, v0_ref, v1_ref, v2_ref, rb_ref,
                 o_ref, bias_ref):
    i = pl.program_id(1)
    tq = TQ_ATTN
    nk = ATTN_KEYS

    @pl.when(i == 0)
    def _build_bias():
        qq = lax.broadcasted_iota(jnp.int32, (tq, nk), 0)
        kk = lax.broadcasted_iota(jnp.int32, (tq, nk), 1)
        band_lo = (qq // CHUNK) * CHUNK
        in_band = (kk >= band_lo) & (kk < band_lo + (A_LEFT_CHUNKS + 1) * CHUNK)
        for h in range(A_HEADS):
            r = jnp.broadcast_to(rb_ref[h:h + 1, :], (tq, BIAS_ROW))
            t = pltpu.roll(r, 0, 1, stride=1, stride_axis=0)
            bias_ref[h] = jnp.where(in_band, t[:, :nk], NEG_BIG)

    kcat = jnp.concatenate([k0_ref[...], k1_ref[...], k2_ref[...]], axis=0)
    vcat = jnp.concatenate([v0_ref[...], v1_ref[...], v2_ref[...]], axis=0)
    kpos = lax.broadcasted_iota(jnp.int32, (1, nk), 1) + (i * tq - A_LEFT_CHUNKS * CHUNK)
    valid = kpos >= 0
    lane = lax.broadcasted_iota(jnp.int32, (1, LANES), 1)
    lo = lane < A_HEAD_DIM
    nt = (((1,), (1,)), ((), ()))
    for p in range(A_HEADS // 2):
        cs = slice(p * LANES, (p + 1) * LANES)
        qp = q_ref[:, cs]
        kp = kcat[:, cs]
        vp = vcat[:, cs]
        acc = None
        inv = None
        for hh in range(2):
            sel = lo if hh == 0 else jnp.logical_not(lo)
            qh = jnp.where(sel, qp, jnp.zeros_like(qp))
            s = lax.dot_general(qh, kp, nt, preferred_element_type=F32)
            s = jnp.where(valid, s + bias_ref[2 * p + hh], NEG_BIG)
            m = jnp.max(s, axis=-1, keepdims=True)
            e = jnp.exp(s - m)
            l = jnp.sum(e, axis=-1, keepdims=True)
            vh = jnp.where(sel, vp, jnp.zeros_like(vp))
            o = jnp.dot(e.astype(BF16), vh, preferred_element_type=F32)
            acc = o if acc is None else acc + o
            r = 1.0 / l
            inv = r if inv is None else jnp.where(lo, inv, r)
        o_ref[:, cs] = (acc * inv).astype(BF16)


def _attention(q, k, v, rb_row, batch):
    t = q.shape[0]
    tq = TQ_ATTN
    nt = t // batch // tq
    cur = pl.BlockSpec((tq, A_WIDTH), lambda b, i: (b * nt + i, 0))
    prev1 = pl.BlockSpec((tq, A_WIDTH), lambda b, i: (b * nt + jnp.maximum(i - 1, 0), 0))
    prev2 = pl.BlockSpec((tq, A_WIDTH), lambda b, i: (b * nt + jnp.maximum(i - 2, 0), 0))
    return pl.pallas_call(
        _attn_kernel,
        grid=(batch, nt),
        in_specs=[cur, prev2, prev1, cur, prev2, prev1, cur,
                  _const_spec((A_HEADS, BIAS_ROW))],
        out_specs=cur,
        out_shape=jax.ShapeDtypeStruct((t, A_WIDTH), BF16),
        scratch_shapes=[pltpu.VMEM((A_HEADS, tq, ATTN_KEYS), F32)],
        compiler_params=_params(("parallel", "arbitrary")),
        name="l0_band_attention",
    )(q, k, k, k, v, v, v, rb_row)


def _mix_ffn_kernel(x_ref, a_ref, b_ref, woa_ref, wob_ref, gpm_ref, gpf_ref, gpo_ref,
                    wg_ref, wu_ref, wd_ref, o_ref):
    m = (jnp.dot(a_ref[...], woa_ref[...], preferred_element_type=F32)
         + jnp.dot(b_ref[...], wob_ref[...], preferred_element_type=F32))
    x1 = x_ref[...] + _rms_norm(m, gpm_ref[...])
    h = _rms_norm(x1, gpf_ref[...]).astype(BF16)
    gate = jnp.dot(h, wg_ref[...], preferred_element_type=F32)
    up = jnp.dot(h, wu_ref[...], preferred_element_type=F32)
    act = (gate * jax.nn.sigmoid(gate) * up).astype(BF16)
    y = jnp.dot(act, wd_ref[...], preferred_element_type=F32)
    o_ref[...] = x1 + _rms_norm(y, gpo_ref[...])


def _mix_ffn(x2d, a, b, wo, g_post_mix, g_pre_ffn, g_post_ffn, w_gate, w_up, w_down):
    t = x2d.shape[0]
    tm = TM_FFN
    half = wo.shape[0] // 2
    tok = lambda n: pl.BlockSpec((tm, n), lambda i: (i, 0))
    vec = _const_spec((1, D_MODEL))
    return pl.pallas_call(
        _mix_ffn_kernel,
        grid=(t // tm,),
        in_specs=[tok(D_MODEL), tok(half), tok(half),
                  _const_spec((half, D_MODEL)), _const_spec((half, D_MODEL)),
                  vec, vec, vec,
                  _const_spec((D_MODEL, D_FF)), _const_spec((D_MODEL, D_FF)),
                  _const_spec((D_FF, D_MODEL))],
        out_specs=tok(D_MODEL),
        out_shape=jax.ShapeDtypeStruct((t, D_MODEL), F32),
        compiler_params=_params(("parallel",)),
        name="mix_out_ffn",
    )(x2d, a, b, wo[:half], wo[half:], g_post_mix, g_pre_ffn, g_post_ffn,
      w_gate, w_up, w_down)


def _l1_in_kernel(x_ref, g_ref, w_ref, wa_ref, wa2_ref, ba_ref,
                  q_ref, k_ref, v_ref, sg_ref, la_ref):
    h = _rms_norm(x_ref[...], g_ref[...]).astype(BF16)
    proj = jnp.dot(h, w_ref[...], preferred_element_type=F32)
    a_low = jnp.dot(h, wa_ref[...], preferred_element_type=F32)
    z = jnp.dot(a_low.astype(BF16), wa2_ref[...], preferred_element_type=F32) + ba_ref[...]
    log_sig = jnp.minimum(z, 0.0) - jnp.log1p(jnp.exp(-jnp.abs(z)))
    la_ref[...] = log_sig * (1.0 / C_GATE_TAU)
    q_ref[...] = proj[:, 0:C_KEY_DIM] * (C_DK ** -0.5)
    k_ref[...] = proj[:, C_KEY_DIM:2 * C_KEY_DIM]
    v_ref[...] = proj[:, 2 * C_KEY_DIM:2 * C_KEY_DIM + C_VAL_DIM].astype(BF16)
    gg = proj[:, 2 * C_KEY_DIM + C_VAL_DIM:C_MAIN]
    sg_ref[...] = (gg * jax.nn.sigmoid(gg)).astype(BF16)


def _l1_in(x2d, g, w_main, w_a, w_a2, b_a):
    t = x2d.shape[0]
    tm = TM_IN1
    tok = lambda n: pl.BlockSpec((tm, n), lambda i: (i, 0))
    return pl.pallas_call(
        _l1_in_kernel,
        grid=(t // tm,),
        in_specs=[tok(D_MODEL), _const_spec((1, D_MODEL)), _const_spec((D_MODEL, C_MAIN)),
                  _const_spec((D_MODEL, C_GATE_RANK)), _const_spec((C_GATE_RANK, C_KEY_DIM)),
                  _const_spec((1, C_KEY_DIM))],
        out_specs=[tok(C_KEY_DIM), tok(C_KEY_DIM), tok(C_VAL_DIM), tok(C_VAL_DIM),
                   tok(C_KEY_DIM)],
        out_shape=[jax.ShapeDtypeStruct((t, C_KEY_DIM), F32),
                   jax.ShapeDtypeStruct((t, C_KEY_DIM), F32),
                   jax.ShapeDtypeStruct((t, C_VAL_DIM), BF16),
                   jax.ShapeDtypeStruct((t, C_VAL_DIM), BF16),
                   jax.ShapeDtypeStruct((t, C_KEY_DIM), F32)],
        compiler_params=_params(("parallel",)),
        name="l1_in_proj_gate",
    )(x2d, g, w_main, w_a, w_a2, b_a)


GLA_LEVELS = (32, 16, 8, 4, 2, 1)


def _level_reference(c_ref, s):
    n = c_ref.shape[1]
    if s >= 4:
        pieces = []
        for p in range(CHUNK // (2 * s)):
            b = 2 * s * p + s - 1
            pieces.append(jnp.broadcast_to(c_ref[b:b + 1, :], (2 * s, n)))
        return jnp.concatenate(pieces, axis=0) if len(pieces) > 1 else pieces[0]
    assert s == 2
    sub = lax.broadcasted_iota(jnp.int32, (8, n), 0)
    pieces = []
    for g in range(CHUNK // 8):
        first = jnp.broadcast_to(c_ref[8 * g + 1:8 * g + 2, :], (8, n))
        second = jnp.broadcast_to(c_ref[8 * g + 5:8 * g + 6, :], (8, n))
        pieces.append(jnp.where(sub < 4, first, second))
    return jnp.concatenate(pieces, axis=0)


def _gla_kernel(q_ref, k_ref, v_ref, sg_ref, la_ref, ng_ref, ya_ref, yb_ref,
                st_ref, c_ref):
    @pl.when(pl.program_id(1) == 0)
    def _reset_state():
        st_ref[...] = jnp.zeros_like(st_ref)

    row = lax.broadcasted_iota(jnp.int32, (CHUNK, CHUNK), 0)
    col = lax.broadcasted_iota(jnp.int32, (CHUNK, CHUNK), 1)
    tri = (col <= row).astype(BF16)
    rowv = lax.broadcasted_iota(jnp.int32, (CHUNK, 1), 0)
    masks = {}
    uppers = {}
    for s in GLA_LEVELS:
        masks[s] = ((row // (2 * s) == col // (2 * s))
                    & ((row // s) % 2 == 1) & ((col // s) % 2 == 0))
        uppers[s] = (rowv // s) % 2 == 1
    nt = (((1,), (1,)), ((), ()))
    tn = (((0,), (0,)), ((), ()))

    def chunk_body(ci, carry):
        r0 = pl.multiple_of(ci * CHUNK, CHUNK)
        rows = pl.ds(r0, CHUNK)
        la = la_ref[rows, :]
        q = q_ref[rows, :]
        k = k_ref[rows, :]
        hi = la.astype(BF16)
        r1 = la - hi.astype(F32)
        mid = r1.astype(BF16)
        low = (r1 - mid.astype(F32)).astype(BF16)
        c = (jnp.dot(tri, hi, preferred_element_type=F32)
             + jnp.dot(tri, mid, preferred_element_type=F32)
             + jnp.dot(tri, low, preferred_element_type=F32))
        c_ref[...] = c
        zs = []
        for s in GLA_LEVELS:
            if s == 1:
                e = jnp.where(uppers[1], la, 0.0)
            else:
                e = -jnp.abs(c - _level_reference(c_ref, s))
            zs.append((jnp.where(uppers[s], q, k) * jnp.exp(e)).astype(BF16))
        c_last = c_ref[CHUNK - 1:CHUNK, :]
        qe = (q * jnp.exp(c)).astype(BF16)
        kd = (k * jnp.exp(c_last - c)).astype(BF16)
        qk = q * k
        state_decay = jnp.exp(c_last)
        for h in range(C_HEADS):
            ks = slice(h * C_DK, (h + 1) * C_DK)
            vs = slice(h * C_DV, (h + 1) * C_DV)
            a = jnp.zeros((CHUNK, CHUNK), F32)
            for li, s in enumerate(GLA_LEVELS):
                zh = zs[li][:, ks]
                gram = lax.dot_general(zh, zh, nt, preferred_element_type=F32)
                a = jnp.where(masks[s], gram, a)
            vh = v_ref[rows, vs]
            st = st_ref[h]
            diag = jnp.sum(qk[:, ks], axis=-1, keepdims=True)
            o = (jnp.dot(a.astype(BF16), vh, preferred_element_type=F32)
                 + diag * vh.astype(F32)
                 + lax.dot_general(qe[:, ks], st.astype(BF16), nt,
                                   preferred_element_type=F32))
            st_ref[h] = (st * state_decay[:, ks]
                         + lax.dot_general(vh, kd[:, ks], tn, preferred_element_type=F32))
            on = o * lax.rsqrt(jnp.mean(o * o, axis=-1, keepdims=True) + NORM_EPS)
            y = (on * ng_ref[:, vs] * sg_ref[rows, vs].astype(F32)).astype(BF16)
            if h < C_HEADS // 2:
                ya_ref[rows, h * C_DV:(h + 1) * C_DV] = y
            else:
                hh = h - C_HEADS // 2
                yb_ref[rows, hh * C_DV:(hh + 1) * C_DV] = y
        return carry

    lax.fori_loop(0, q_ref.shape[0] // CHUNK, chunk_body, 0)


def _gla(q, k, v, sg, la, norm_g, batch):
    t = q.shape[0]
    tc = TC_GLA
    nt = t // batch // tc
    tok = lambda n: pl.BlockSpec((tc, n), lambda b, i: (b * nt + i, 0))
    half = C_VAL_DIM // 2
    out = jax.ShapeDtypeStruct((t, half), BF16)
    return pl.pallas_call(
        _gla_kernel,
        grid=(batch, nt),
        in_specs=[tok(C_KEY_DIM), tok(C_KEY_DIM), tok(C_VAL_DIM), tok(C_VAL_DIM),
                  tok(C_KEY_DIM), _const_spec((1, C_VAL_DIM))],
        out_specs=[tok(half), tok(half)],
        out_shape=[out, out],
        scratch_shapes=[pltpu.VMEM((C_HEADS, C_DV, C_DK), F32),
                        pltpu.VMEM((CHUNK, C_KEY_DIM), F32)],
        compiler_params=_params(("parallel", "arbitrary")),
        name="l1_gla_scan",
    )(q, k, v, sg, la, norm_g)


def _rel_bias_row(rel_bias):
    h = rel_bias.shape[0]
    last = rel_bias[:, 2 * A_MAX_REL:]
    head = jnp.broadcast_to(last, (h, A_MAX_REL))
    mid = rel_bias[:, ::-1]
    tail = jnp.broadcast_to(last, (h, BIAS_ROW - 3 * A_MAX_REL - 1))
    return jnp.concatenate([head, mid, tail], axis=1)


def kernel(x, pre_mix_g, post_mix_g, pre_ffn_g, post_ffn_g, ab_w_in, a_rel_bias, b_ln_g,
           b_ln_b, b_w_s, b_b_s, ab_w_out, c_w_in, c_w_a2, c_b_a, c_norm_g, c_w_out,
           ffn_w_gate, ffn_w_up, ffn_w_down):
    batch, seq, d = x.shape
    x2d = x.reshape(batch * seq, d)
    vec = lambda a: a.reshape(1, -1)
    bf = lambda a: a.astype(BF16)

    q, k, v, b_out = _l0_in(x2d, vec(pre_mix_g[0]), bf(ab_w_in[0]), vec(b_ln_g[0]),
                            vec(b_ln_b[0]), b_w_s[0], b_b_s[0].T)
    a_out = _attention(q, k, v, _rel_bias_row(a_rel_bias[0]), batch)
    x2d = _mix_ffn(x2d, a_out, b_out, bf(ab_w_out[0]), vec(post_mix_g[0]),
                   vec(pre_ffn_g[0]), vec(post_ffn_g[0]), bf(ffn_w_gate[0]),
                   bf(ffn_w_up[0]), bf(ffn_w_down[0]))

    w_in1 = c_w_in[0]
    q, k, v, sg, la = _l1_in(x2d, vec(pre_mix_g[1]), bf(w_in1[:, :C_MAIN]),
                             bf(w_in1[:, C_MAIN:]), bf(c_w_a2[0]), vec(c_b_a[0]))
    ya, yb = _gla(q, k, v, sg, la, vec(c_norm_g[0]), batch)
    x2d = _mix_ffn(x2d, ya, yb, bf(c_w_out[0]), vec(post_mix_g[1]), vec(pre_ffn_g[1]),
                   vec(post_ffn_g[1]), bf(ffn_w_gate[1]), bf(ffn_w_up[1]),
                   bf(ffn_w_down[1]))
    return x2d.reshape(batch, seq, d)
```

```python
import functools

import jax
import jax.numpy as jnp
from jax import lax
from jax.experimental import pallas as pl
from jax.experimental.pallas import tpu as pltpu

F32 = jnp.float32
BF16 = jnp.bfloat16

D_MODEL = 1024
CHUNK = 64
NORM_EPS = 1e-6
A_HEADS = 8
A_HEAD_DIM = 64
A_WIDTH = A_HEADS * A_HEAD_DIM
A_LEFT_CHUNKS = 8
A_MAX_REL = 256
B_GROUPS = 4
B_BLOCK = 128
B_WIDTH = D_MODEL // 2
AB_IN = 3 * A_WIDTH + 2 * B_WIDTH
C_HEADS = 4
C_KEY_DIM = D_MODEL // 2
C_VAL_DIM = D_MODEL
C_DK = C_KEY_DIM // C_HEADS
C_DV = C_VAL_DIM // C_HEADS
C_GATE_RANK = 16
C_GATE_TAU = 16.0
C_MAIN = 2 * C_KEY_DIM + 2 * C_VAL_DIM
D_FF = 2816

LANES = 128
VMEM_LIMIT_BYTES = 56 * 1024 * 1024

NEG_BIG = float(jnp.finfo(jnp.float32).min)
LOG2E = 1.4426950408889634

TM_IN0 = 512
TQ_ATTN = 256
TM_FFN = 512
TM_IN1 = 512
TC_GLA = 256
GLA_BLOCK = 128
ATTN_KEYS = TQ_ATTN + A_LEFT_CHUNKS * CHUNK
BIAS_ROW = 1024


def _rms_norm(x, g):
    ms = jnp.mean(x * x, axis=-1, keepdims=True)
    return x * lax.rsqrt(ms + NORM_EPS) * g


def _gelu(x):
    return 0.5 * x * (1.0 + lax.erf(x * (2.0 ** -0.5)))


def _const_spec(shape):
    nd = len(shape)
    return pl.BlockSpec(shape, lambda *_: (0,) * nd, pipeline_mode=pl.Buffered(1))


def _params(sem):
    return pltpu.CompilerParams(dimension_semantics=sem,
                                vmem_limit_bytes=VMEM_LIMIT_BYTES)


def _l0_in_kernel(x_ref, g_ref, w_ref, lng_ref, lnb_ref, ws_ref, bst_ref,
                  q_ref, k_ref, v_ref, b_ref):
    tm = x_ref.shape[0]
    h = _rms_norm(x_ref[...], g_ref[...]).astype(BF16)
    proj = jnp.dot(h, w_ref[...], preferred_element_type=F32)
    q_ref[...] = (proj[:, 0:A_WIDTH] * (A_HEAD_DIM ** -0.5 * LOG2E)).astype(BF16)
    k_ref[...] = proj[:, A_WIDTH:2 * A_WIDTH].astype(BF16)
    v_ref[...] = proj[:, 2 * A_WIDTH:3 * A_WIDTH].astype(BF16)
    u = _gelu(proj[:, 3 * A_WIDTH:3 * A_WIDTH + B_WIDTH])
    zv = _gelu(proj[:, 3 * A_WIDTH + B_WIDTH:AB_IN])
    mu = jnp.mean(zv, axis=-1, keepdims=True)
    d = zv - mu
    var = jnp.mean(d * d, axis=-1, keepdims=True)
    vln = (d * lax.rsqrt(var + NORM_EPS) * lng_ref[...] + lnb_ref[...]).astype(BF16)
    row = lax.broadcasted_iota(jnp.int32, (B_BLOCK, B_BLOCK), 0)
    col = lax.broadcasted_iota(jnp.int32, (B_BLOCK, B_BLOCK), 1)
    causal = col <= row
    for g in range(B_GROUPS):
        wg = jnp.where(causal, ws_ref[g], 0.0).astype(BF16)
        bcol = bst_ref[:, g:g + 1]
        cs = slice(g * LANES, (g + 1) * LANES)
        for n in range(tm // B_BLOCK):
            rs = slice(n * B_BLOCK, (n + 1) * B_BLOCK)
            f = jnp.dot(wg, vln[rs, cs], preferred_element_type=F32) + bcol
            b_ref[rs, cs] = (u[rs, cs] * f).astype(BF16)


def _l0_in(x2d, g, w_in, ln_g, ln_b, w_s, b_s_t):
    t = x2d.shape[0]
    tm = TM_IN0
    tok = lambda n: pl.BlockSpec((tm, n), lambda i: (i, 0))
    out = jax.ShapeDtypeStruct((t, A_WIDTH), BF16)
    return pl.pallas_call(
        _l0_in_kernel,
        grid=(t // tm,),
        in_specs=[tok(D_MODEL), _const_spec((1, D_MODEL)), _const_spec((D_MODEL, AB_IN)),
                  _const_spec((1, B_WIDTH)), _const_spec((1, B_WIDTH)),
                  _const_spec((B_GROUPS, B_BLOCK, B_BLOCK)), _const_spec((B_BLOCK, B_GROUPS))],
        out_specs=[tok(A_WIDTH)] * 4,
        out_shape=[out] * 4,
        compiler_params=_params(("parallel",)),
        name="l0_in_proj_gmlp",
    )(x2d, g, w_in, ln_g, ln_b, w_s, b_s_t)


def _attn_kernel(q_ref, k0_ref, k1_ref, k2_ref, v0_ref, v1_ref, v2_ref, rb_ref,
                 o_ref, bias_ref):
    i = pl.program_id(1)
    tq = TQ_ATTN
    nk = ATTN_KEYS

    @pl.when(i * tq <= A_LEFT_CHUNKS * CHUNK)
    def _build_bias():
        qq = lax.broadcasted_iota(jnp.int32, (tq, nk), 0)
        kk = lax.broadcasted_iota(jnp.int32, (tq, nk), 1)
        band_lo = (qq // CHUNK) * CHUNK
        keep = ((kk >= band_lo) & (kk < band_lo + (A_LEFT_CHUNKS + 1) * CHUNK)
                & (kk + (i * tq - A_LEFT_CHUNKS * CHUNK) >= 0))
        for h in range(A_HEADS):
            r = jnp.broadcast_to(rb_ref[h:h + 1, :], (tq, BIAS_ROW))
            t = pltpu.roll(r, 0, 1, stride=1, stride_axis=0)
            bias_ref[h] = jnp.where(keep, t[:, :nk] * LOG2E, NEG_BIG)

    kcat = jnp.concatenate([k0_ref[...], k1_ref[...], k2_ref[...]], axis=0)
    vcat = jnp.concatenate([v0_ref[...], v1_ref[...], v2_ref[...]], axis=0)
    lane = lax.broadcasted_iota(jnp.int32, (1, LANES), 1)
    lo = lane < A_HEAD_DIM
    nt = (((1,), (1,)), ((), ()))
    for p in range(A_HEADS // 2):
        cs = slice(p * LANES, (p + 1) * LANES)
        qp = q_ref[:, cs]
        kp = kcat[:, cs]
        vp = vcat[:, cs]
        outs = []
        for hh in range(2):
            sel = lo if hh == 0 else jnp.logical_not(lo)
            ones_lane = A_HEAD_DIM if hh == 0 else 0
            qh = jnp.where(sel, qp, jnp.zeros_like(qp))
            s = lax.dot_general(qh, kp, nt, preferred_element_type=F32)
            s = s + bias_ref[2 * p + hh]
            m = jnp.max(s, axis=-1, keepdims=True)
            e = jnp.exp2(s - m).astype(BF16)
            vh = jnp.where(sel, vp, (lane == ones_lane).astype(BF16))
            o = jnp.dot(e, vh, preferred_element_type=F32)
            outs.append(o / o[:, ones_lane:ones_lane + 1])
        o_ref[:, cs] = jnp.where(lo, outs[0], outs[1]).astype(BF16)


def _attention(q, k, v, rb_row, batch):
    t = q.shape[0]
    tq = TQ_ATTN
    nt = t // batch // tq
    cur = pl.BlockSpec((tq, A_WIDTH), lambda b, i: (b * nt + i, 0))
    prev1 = pl.BlockSpec((tq, A_WIDTH), lambda b, i: (b * nt + jnp.maximum(i - 1, 0), 0))
    prev2 = pl.BlockSpec((tq, A_WIDTH), lambda b, i: (b * nt + jnp.maximum(i - 2, 0), 0))
    return pl.pallas_call(
        _attn_kernel,
        grid=(batch, nt),
        in_specs=[cur, prev2, prev1, cur, prev2, prev1, cur,
                  _const_spec((A_HEADS, BIAS_ROW))],
        out_specs=cur,
        out_shape=jax.ShapeDtypeStruct((t, A_WIDTH), BF16),
        scratch_shapes=[pltpu.VMEM((A_HEADS, tq, ATTN_KEYS), F32)],
        compiler_params=_params(("parallel", "arbitrary")),
        name="l0_band_attention",
    )(q, k, k, k, v, v, v, rb_row)


def _mix_ffn_kernel(x_ref, a_ref, b_ref, woa_ref, wob_ref, gpm_ref, gpf_ref, gpo_ref,
                    wg_ref, wu_ref, wd_ref, o_ref):
    m = (jnp.dot(a_ref[...], woa_ref[...], preferred_element_type=F32)
         + jnp.dot(b_ref[...], wob_ref[...], preferred_element_type=F32))
    x1 = x_ref[...] + _rms_norm(m, gpm_ref[...])
    h = _rms_norm(x1, gpf_ref[...]).astype(BF16)
    gate = jnp.dot(h, wg_ref[...], preferred_element_type=F32)
    up = jnp.dot(h, wu_ref[...], preferred_element_type=F32)
    act = (gate * jax.nn.sigmoid(gate) * up).astype(BF16)
    y = jnp.dot(act, wd_ref[...], preferred_element_type=F32)
    o_ref[...] = x1 + _rms_norm(y, gpo_ref[...])


def _mix_ffn(x2d, a, b, wo, g_post_mix, g_pre_ffn, g_post_ffn, w_gate, w_up, w_down):
    t = x2d.shape[0]
    tm = TM_FFN
    half = wo.shape[0] // 2
    tok = lambda n: pl.BlockSpec((tm, n), lambda i: (i, 0))
    vec = _const_spec((1, D_MODEL))
    return pl.pallas_call(
        _mix_ffn_kernel,
        grid=(t // tm,),
        in_specs=[tok(D_MODEL), tok(half), tok(half),
                  _const_spec((half, D_MODEL)), _const_spec((half, D_MODEL)),
                  vec, vec, vec,
                  _const_spec((D_MODEL, D_FF)), _const_spec((D_MODEL, D_FF)),
                  _const_spec((D_FF, D_MODEL))],
        out_specs=tok(D_MODEL),
        out_shape=jax.ShapeDtypeStruct((t, D_MODEL), F32),
        compiler_params=_params(("parallel",)),
        name="mix_out_ffn",
    )(x2d, a, b, wo[:half], wo[half:], g_post_mix, g_pre_ffn, g_post_ffn,
      w_gate, w_up, w_down)


def _l1_in_kernel(x_ref, g_ref, w_ref, wa_ref, wa2_ref, ba_ref,
                  q_ref, k_ref, v_ref, sg_ref, la_ref):
    h = _rms_norm(x_ref[...], g_ref[...]).astype(BF16)
    proj = jnp.dot(h, w_ref[...], preferred_element_type=F32)
    a_low = jnp.dot(h, wa_ref[...], preferred_element_type=F32)
    z = jnp.dot(a_low.astype(BF16), wa2_ref[...], preferred_element_type=F32) + ba_ref[...]
    log_sig = jnp.minimum(z, 0.0) - jnp.log1p(jnp.exp(-jnp.abs(z)))
    la_ref[...] = log_sig * (1.0 / C_GATE_TAU)
    q_ref[...] = proj[:, 0:C_KEY_DIM] * (C_DK ** -0.5)
    k_ref[...] = proj[:, C_KEY_DIM:2 * C_KEY_DIM]
    v_ref[...] = proj[:, 2 * C_KEY_DIM:2 * C_KEY_DIM + C_VAL_DIM].astype(BF16)
    gg = proj[:, 2 * C_KEY_DIM + C_VAL_DIM:C_MAIN]
    sg_ref[...] = (gg * jax.nn.sigmoid(gg)).astype(BF16)


def _l1_in(x2d, g, w_main, w_a, w_a2, b_a):
    t = x2d.shape[0]
    tm = TM_IN1
    tok = lambda n: pl.BlockSpec((tm, n), lambda i: (i, 0))
    return pl.pallas_call(
        _l1_in_kernel,
        grid=(t // tm,),
        in_specs=[tok(D_MODEL), _const_spec((1, D_MODEL)), _const_spec((D_MODEL, C_MAIN)),
                  _const_spec((D_MODEL, C_GATE_RANK)), _const_spec((C_GATE_RANK, C_KEY_DIM)),
                  _const_spec((1, C_KEY_DIM))],
        out_specs=[tok(C_KEY_DIM), tok(C_KEY_DIM), tok(C_VAL_DIM), tok(C_VAL_DIM),
                   tok(C_KEY_DIM)],
        out_shape=[jax.ShapeDtypeStruct((t, C_KEY_DIM), F32),
                   jax.ShapeDtypeStruct((t, C_KEY_DIM), F32),
                   jax.ShapeDtypeStruct((t, C_VAL_DIM), BF16),
                   jax.ShapeDtypeStruct((t, C_VAL_DIM), BF16),
                   jax.ShapeDtypeStruct((t, C_KEY_DIM), F32)],
        compiler_params=_params(("parallel",)),
        name="l1_in_proj_gate",
    )(x2d, g, w_main, w_a, w_a2, b_a)


def _gla_levels(block):
    levels = []
    s = block // 2
    while s >= 1:
        levels.append(s)
        s //= 2
    return tuple(levels)


GLA_MXU_LEVELS = (4, 2)


def _range_sum_matrix(block):
    row = lax.broadcasted_iota(jnp.int32, (block, block), 0)
    col = lax.broadcasted_iota(jnp.int32, (block, block), 1)
    mats = [(col <= row).astype(F32)]
    for s in GLA_MXU_LEVELS:
        bnd = (row & (-2 * s)) + (s - 1)
        upper = (row & s) != 0
        first = jnp.where(upper, bnd, row)
        last = jnp.where(upper, row, bnd)
        mats.append(((col > first) & (col <= last)).astype(F32))
    return jnp.concatenate(mats, axis=0).astype(BF16)


def _level_factors(c_ref, c2, small_e, q, k, r0, block, s):
    if s >= 8:
        parts = []
        for p in range(block // (2 * s)):
            lo = slice(2 * s * p, 2 * s * p + s)
            up = slice(2 * s * p + s, 2 * s * (p + 1))
            b = r0 + 2 * s * p + s - 1
            ref_row = c_ref[b:b + 1, :]
            parts.append(k[lo] * jnp.exp2(ref_row - c2[lo]))
            parts.append(q[up] * jnp.exp2(c2[up] - ref_row))
        return jnp.concatenate(parts, axis=0)
    rowv = lax.broadcasted_iota(jnp.int32, (block, 1), 0)
    upper = (rowv & s) != 0
    return jnp.where(upper, q, k) * jnp.exp2(small_e[s])


def _gla_kernel(q_ref, k_ref, v_ref, sg_ref, la_ref, ng_ref, ya_ref, yb_ref,
                st_ref, c_ref, m_ref):
    blk = GLA_BLOCK

    @pl.when(pl.program_id(1) == 0)
    def _reset():
        st_ref[...] = jnp.zeros_like(st_ref)
        m_ref[...] = _range_sum_matrix(blk)

    levels = _gla_levels(blk)
    row = lax.broadcasted_iota(jnp.int32, (blk, blk), 0)
    col = lax.broadcasted_iota(jnp.int32, (blk, blk), 1)
    eye = col == row
    split = jnp.where(row > col, row ^ col, 0)
    odd = (lax.broadcasted_iota(jnp.int32, (blk, 1), 0) & 1) != 0
    nt = (((1,), (1,)), ((), ()))
    tn = (((0,), (0,)), ((), ()))

    for ci in range(q_ref.shape[0] // blk):
        r0 = ci * blk
        rows = slice(r0, r0 + blk)
        q = q_ref[rows, :]
        k = k_ref[rows, :]
        la2 = la_ref[rows, :] * LOG2E
        hi = la2.astype(BF16)
        mid = (la2 - hi.astype(F32)).astype(BF16)
        sums = (jnp.dot(m_ref[...], hi, preferred_element_type=F32)
                + jnp.dot(m_ref[...], mid, preferred_element_type=F32))
        c2 = sums[0:blk]
        small_e = {1: jnp.where(odd, la2, 0.0)}
        for n, s in enumerate(GLA_MXU_LEVELS):
            small_e[s] = sums[(n + 1) * blk:(n + 2) * blk]
        c_ref[rows, :] = c2
        zs = [_level_factors(c_ref, c2, small_e, q, k, r0, blk, s).astype(BF16)
              for s in levels]
        c_last = c_ref[r0 + blk - 1:r0 + blk, :]
        qe = (q * jnp.exp2(c2)).astype(BF16)
        kd = (k * jnp.exp2(c_last - c2)).astype(BF16)
        qk = q * k
        state_decay = jnp.exp2(c_last)
        for h in range(C_HEADS):
            ks = slice(h * C_DK, (h + 1) * C_DK)
            vs = slice(h * C_DV, (h + 1) * C_DV)
            diag = jnp.sum(qk[:, ks], axis=-1, keepdims=True)
            a = jnp.where(eye, diag, 0.0)
            for li in reversed(range(len(levels))):
                zh = zs[li][:, ks]
                gram = lax.dot_general(zh, zh, nt, preferred_element_type=F32)
                a = jnp.where(split >= levels[li], gram, a)
            vh = v_ref[rows, vs]
            st = st_ref[h]
            o = (jnp.dot(a.astype(BF16), vh, preferred_element_type=F32)
                 + lax.dot_general(qe[:, ks], st.astype(BF16), nt,
                                   preferred_element_type=F32))
            st_ref[h] = (st * state_decay[:, ks]
                         + lax.dot_general(vh, kd[:, ks], tn, preferred_element_type=F32))
            on = o * lax.rsqrt(jnp.mean(o * o, axis=-1, keepdims=True) + NORM_EPS)
            y = (on * ng_ref[:, vs] * sg_ref[rows, vs].astype(F32)).astype(BF16)
            if h < C_HEADS // 2:
                ya_ref[rows, h * C_DV:(h + 1) * C_DV] = y
            else:
                hh = h - C_HEADS // 2
                yb_ref[rows, hh * C_DV:(hh + 1) * C_DV] = y


def _gla(q, k, v, sg, la, norm_g, batch):
    t = q.shape[0]
    tc = TC_GLA
    nt = t // batch // tc
    tok = lambda n: pl.BlockSpec((tc, n), lambda b, i: (b * nt + i, 0))
    half = C_VAL_DIM // 2
    out = jax.ShapeDtypeStruct((t, half), BF16)
    return pl.pallas_call(
        _gla_kernel,
        grid=(batch, nt),
        in_specs=[tok(C_KEY_DIM), tok(C_KEY_DIM), tok(C_VAL_DIM), tok(C_VAL_DIM),
                  tok(C_KEY_DIM), _const_spec((1, C_VAL_DIM))],
        out_specs=[tok(half), tok(half)],
        out_shape=[out, out],
        scratch_shapes=[pltpu.VMEM((C_HEADS, C_DV, C_DK), F32),
                        pltpu.VMEM((tc, C_KEY_DIM), F32),
                        pltpu.VMEM(((1 + len(GLA_MXU_LEVELS)) * GLA_BLOCK, GLA_BLOCK), BF16)],
        compiler_params=_params(("parallel", "arbitrary")),
        name="l1_gla_scan",
    )(q, k, v, sg, la, norm_g)


def _rel_bias_row(rel_bias):
    h = rel_bias.shape[0]
    last = rel_bias[:, 2 * A_MAX_REL:]
    head = jnp.broadcast_to(last, (h, A_MAX_REL))
    mid = rel_bias[:, ::-1]
    tail = jnp.broadcast_to(last, (h, BIAS_ROW - 3 * A_MAX_REL - 1))
    return jnp.concatenate([head, mid, tail], axis=1)


def kernel(x, pre_mix_g, post_mix_g, pre_ffn_g, post_ffn_g, ab_w_in, a_rel_bias, b_ln_g,
           b_ln_b, b_w_s, b_b_s, ab_w_out, c_w_in, c_w_a2, c_b_a, c_norm_g, c_w_out,
           ffn_w_gate, ffn_w_up, ffn_w_down):
    batch, seq, d = x.shape
    x2d = x.reshape(batch * seq, d)
    vec = lambda a: a.reshape(1, -1)
    bf = lambda a: a.astype(BF16)

    q, k, v, b_out = _l0_in(x2d, vec(pre_mix_g[0]), bf(ab_w_in[0]), vec(b_ln_g[0]),
                            vec(b_ln_b[0]), b_w_s[0], b_b_s[0].T)
    a_out = _attention(q, k, v, _rel_bias_row(a_rel_bias[0]), batch)
    x2d = _mix_ffn(x2d, a_out, b_out, bf(ab_w_out[0]), vec(post_mix_g[0]),
                   vec(pre_ffn_g[0]), vec(post_ffn_g[0]), bf(ffn_w_gate[0]),
                   bf(ffn_w_up[0]), bf(ffn_w_down[0]))

    w_in1 = c_w_in[0]
    q, k, v, sg, la = _l1_in(x2d, vec(pre_mix_g[1]), bf(w_in1[:, :C_MAIN]),
                             bf(w_in1[:, C_MAIN:]), bf(c_w_a2[0]), vec(c_b_a[0]))
    ya, yb = _gla(q, k, v, sg, la, vec(c_norm_g[0]), batch)
    x2d = _mix_ffn(x2d, ya, yb, bf(c_w_out[0]), vec(post_mix_g[1]), vec(pre_ffn_g[1]),
                   vec(post_ffn_g[1]), bf(ffn_w_gate[1]), bf(ffn_w_up[1]),
                   bf(ffn_w_down[1]))
    return x2d.reshape(batch, seq, d)
```

```python
import functools

import jax
import jax.numpy as jnp
from jax import lax
from jax.experimental import pallas as pl
from jax.experimental.pallas import tpu as pltpu

F32 = jnp.float32
BF16 = jnp.bfloat16

D_MODEL = 1024
CHUNK = 64
NORM_EPS = 1e-6
A_HEADS = 8
A_HEAD_DIM = 64
A_WIDTH = A_HEADS * A_HEAD_DIM
A_LEFT_CHUNKS = 8
A_MAX_REL = 256
B_GROUPS = 4
B_BLOCK = 128
B_WIDTH = D_MODEL // 2
AB_IN = 3 * A_WIDTH + 2 * B_WIDTH
C_HEADS = 4
C_KEY_DIM = D_MODEL // 2
C_VAL_DIM = D_MODEL
C_DK = C_KEY_DIM // C_HEADS
C_DV = C_VAL_DIM // C_HEADS
C_GATE_RANK = 16
C_GATE_TAU = 16.0
C_MAIN = 2 * C_KEY_DIM + 2 * C_VAL_DIM
D_FF = 2816

LANES = 128
VMEM_LIMIT_BYTES = 56 * 1024 * 1024

NEG_BIG = float(jnp.finfo(jnp.float32).min)
LOG2E = 1.4426950408889634

TM_IN0 = 1024
IN_SUBTILES = 4
TQ_ATTN = 256
TM_FFN = 512
FFN_SUBTILES = 2
TM_IN1 = 1024
TC_GLA = 256
GLA_BLOCK = 128
ATTN_KEYS = TQ_ATTN + A_LEFT_CHUNKS * CHUNK
BIAS_ROW = 1024


def _rms_norm(x, g):
    ms = jnp.mean(x * x, axis=-1, keepdims=True)
    return x * lax.rsqrt(ms + NORM_EPS) * g


def _gelu(x):
    return 0.5 * x * (1.0 + lax.erf(x * (2.0 ** -0.5)))


def _const_spec(shape):
    nd = len(shape)
    return pl.BlockSpec(shape, lambda *_: (0,) * nd, pipeline_mode=pl.Buffered(1))


def _params(sem):
    return pltpu.CompilerParams(dimension_semantics=sem,
                                vmem_limit_bytes=VMEM_LIMIT_BYTES)


def _l0_in_kernel(x_ref, g_ref, w_ref, lng_ref, lnb_ref, ws_ref, bst_ref,
                  q_ref, k_ref, v_ref, b_ref):
    sub = x_ref.shape[0] // IN_SUBTILES
    row = lax.broadcasted_iota(jnp.int32, (B_BLOCK, B_BLOCK), 0)
    col = lax.broadcasted_iota(jnp.int32, (B_BLOCK, B_BLOCK), 1)
    causal = col <= row
    w_gate = [jnp.where(causal, ws_ref[g], 0.0).astype(BF16) for g in range(B_GROUPS)]

    def project(r):
        rows = slice(r * sub, (r + 1) * sub)
        h = _rms_norm(x_ref[rows, :], g_ref[...]).astype(BF16)
        return jnp.dot(h, w_ref[...], preferred_element_type=F32)

    def finish(r, proj):
        rows = slice(r * sub, (r + 1) * sub)
        q_ref[rows, :] = (proj[:, 0:A_WIDTH] * (A_HEAD_DIM ** -0.5 * LOG2E)).astype(BF16)
        k_ref[rows, :] = proj[:, A_WIDTH:2 * A_WIDTH].astype(BF16)
        v_ref[rows, :] = proj[:, 2 * A_WIDTH:3 * A_WIDTH].astype(BF16)
        u = _gelu(proj[:, 3 * A_WIDTH:3 * A_WIDTH + B_WIDTH])
        zv = _gelu(proj[:, 3 * A_WIDTH + B_WIDTH:AB_IN])
        mu = jnp.mean(zv, axis=-1, keepdims=True)
        d = zv - mu
        var = jnp.mean(d * d, axis=-1, keepdims=True)
        vln = (d * lax.rsqrt(var + NORM_EPS) * lng_ref[...] + lnb_ref[...]).astype(BF16)
        for g in range(B_GROUPS):
            bcol = bst_ref[:, g:g + 1]
            cs = slice(g * LANES, (g + 1) * LANES)
            for n in range(sub // B_BLOCK):
                rs = slice(n * B_BLOCK, (n + 1) * B_BLOCK)
                f = jnp.dot(w_gate[g], vln[rs, cs], preferred_element_type=F32) + bcol
                out_rows = slice(r * sub + n * B_BLOCK, r * sub + (n + 1) * B_BLOCK)
                b_ref[out_rows, cs] = (u[rs, cs] * f).astype(BF16)

    proj = project(0)
    for r in range(IN_SUBTILES):
        upcoming = project(r + 1) if r + 1 < IN_SUBTILES else None
        finish(r, proj)
        proj = upcoming


def _l0_in(x2d, g, w_in, ln_g, ln_b, w_s, b_s_t):
    t = x2d.shape[0]
    tm = TM_IN0
    tok = lambda n: pl.BlockSpec((tm, n), lambda i: (i, 0))
    out = jax.ShapeDtypeStruct((t, A_WIDTH), BF16)
    return pl.pallas_call(
        _l0_in_kernel,
        grid=(t // tm,),
        in_specs=[tok(D_MODEL), _const_spec((1, D_MODEL)), _const_spec((D_MODEL, AB_IN)),
                  _const_spec((1, B_WIDTH)), _const_spec((1, B_WIDTH)),
                  _const_spec((B_GROUPS, B_BLOCK, B_BLOCK)), _const_spec((B_BLOCK, B_GROUPS))],
        out_specs=[tok(A_WIDTH)] * 4,
        out_shape=[out] * 4,
        compiler_params=_params(("parallel",)),
        name="l0_in_proj_gmlp",
    )(x2d, g, w_in, ln_g, ln_b, w_s, b_s_t)


def _attn_kernel(q_ref, k0_ref, k1_ref, k2_ref, v0_ref, v1_ref, v2_ref, rb_ref,
                 o_ref, bias_ref):
    i = pl.program_id(1)
    tq = TQ_ATTN
    nk = ATTN_KEYS

    @pl.when(i * tq <= A_LEFT_CHUNKS * CHUNK)
    def _build_bias():
        qq = lax.broadcasted_iota(jnp.int32, (tq, nk), 0)
        kk = lax.broadcasted_iota(jnp.int32, (tq, nk), 1)
        band_lo = (qq // CHUNK) * CHUNK
        keep = ((kk >= band_lo) & (kk < band_lo + (A_LEFT_CHUNKS + 1) * CHUNK)
                & (kk + (i * tq - A_LEFT_CHUNKS * CHUNK) >= 0))
        for h in range(A_HEADS):
            r = jnp.broadcast_to(rb_ref[h:h + 1, :], (tq, BIAS_ROW))
            t = pltpu.roll(r, 0, 1, stride=1, stride_axis=0)
            bias_ref[h] = jnp.where(keep, t[:, :nk] * LOG2E, NEG_BIG)

    kcat = jnp.concatenate([k0_ref[...], k1_ref[...], k2_ref[...]], axis=0)
    vcat = jnp.concatenate([v0_ref[...], v1_ref[...], v2_ref[...]], axis=0)
    lane = lax.broadcasted_iota(jnp.int32, (1, LANES), 1)
    lo = lane < A_HEAD_DIM
    nt = (((1,), (1,)), ((), ()))
    def head_sel(h):
        return lo if h % 2 == 0 else jnp.logical_not(lo)

    def scores(h):
        cs = slice((h // 2) * LANES, (h // 2 + 1) * LANES)
        qp = q_ref[:, cs]
        qh = jnp.where(head_sel(h), qp, jnp.zeros_like(qp))
        return lax.dot_general(qh, kcat[:, cs], nt, preferred_element_type=F32)

    def softmax_numerator(h, s):
        s = s + bias_ref[h]
        m = jnp.max(s, axis=-1, keepdims=True)
        return jnp.exp2(s - m).astype(BF16)

    def weighted_values(h, e):
        cs = slice((h // 2) * LANES, (h // 2 + 1) * LANES)
        ones_lane = A_HEAD_DIM if h % 2 == 0 else 0
        vh = jnp.where(head_sel(h), vcat[:, cs], (lane == ones_lane).astype(BF16))
        o = jnp.dot(e, vh, preferred_element_type=F32)
        return o / o[:, ones_lane:ones_lane + 1]

    s_val = {}
    e_val = {}
    outs = {}
    for t in range(A_HEADS + 2):
        if t < A_HEADS:
            s_val[t] = scores(t)
        if 0 <= t - 1 < A_HEADS:
            e_val[t - 1] = softmax_numerator(t - 1, s_val.pop(t - 1))
        h = t - 2
        if 0 <= h < A_HEADS:
            outs[h] = weighted_values(h, e_val.pop(h))
            if h % 2 == 1:
                cs = slice((h // 2) * LANES, (h // 2 + 1) * LANES)
                o_ref[:, cs] = jnp.where(lo, outs.pop(h - 1), outs.pop(h)).astype(BF16)


def _attention(q, k, v, rb_row, batch):
    t = q.shape[0]
    tq = TQ_ATTN
    nt = t // batch // tq
    cur = pl.BlockSpec((tq, A_WIDTH), lambda b, i: (b * nt + i, 0))
    prev1 = pl.BlockSpec((tq, A_WIDTH), lambda b, i: (b * nt + jnp.maximum(i - 1, 0), 0))
    prev2 = pl.BlockSpec((tq, A_WIDTH), lambda b, i: (b * nt + jnp.maximum(i - 2, 0), 0))
    return pl.pallas_call(
        _attn_kernel,
        grid=(batch, nt),
        in_specs=[cur, prev2, prev1, cur, prev2, prev1, cur,
                  _const_spec((A_HEADS, BIAS_ROW))],
        out_specs=cur,
        out_shape=jax.ShapeDtypeStruct((t, A_WIDTH), BF16),
        scratch_shapes=[pltpu.VMEM((A_HEADS, tq, ATTN_KEYS), F32)],
        compiler_params=_params(("parallel", "arbitrary")),
        name="l0_band_attention",
    )(q, k, k, k, v, v, v, rb_row)


def _mix_ffn_kernel(x_ref, a_ref, b_ref, woa_ref, wob_ref, gpm_ref, gpf_ref, gpo_ref,
                    wg_ref, wu_ref, wd_ref, o_ref):
    sub = x_ref.shape[0] // FFN_SUBTILES
    tiles = [slice(r * sub, (r + 1) * sub) for r in range(FFN_SUBTILES)]

    def mix_proj(rows):
        return (jnp.dot(a_ref[rows, :], woa_ref[...], preferred_element_type=F32)
                + jnp.dot(b_ref[rows, :], wob_ref[...], preferred_element_type=F32))

    def norms(rows, m):
        x1 = x_ref[rows, :] + _rms_norm(m, gpm_ref[...])
        return x1, _rms_norm(x1, gpf_ref[...]).astype(BF16)

    def gate_up(h):
        return (jnp.dot(h, wg_ref[...], preferred_element_type=F32),
                jnp.dot(h, wu_ref[...], preferred_element_type=F32))

    def activation(gu):
        gate, up = gu
        return (gate * jax.nn.sigmoid(gate) * up).astype(BF16)

    def down(act):
        return jnp.dot(act, wd_ref[...], preferred_element_type=F32)

    def finish(rows, x1, y):
        o_ref[rows, :] = x1 + _rms_norm(y, gpo_ref[...])

    stages = [mix_proj, norms, gate_up, activation, down, finish]
    state = [None] * FFN_SUBTILES
    x1s = [None] * FFN_SUBTILES
    for step in range(len(stages) + FFN_SUBTILES - 1):
        for r in range(FFN_SUBTILES):
            t = step - r
            if t < 0 or t >= len(stages):
                continue
            rows = tiles[r]
            if t == 0:
                state[r] = mix_proj(rows)
            elif t == 1:
                x1s[r], state[r] = norms(rows, state[r])
            elif t == 2:
                state[r] = gate_up(state[r])
            elif t == 3:
                state[r] = activation(state[r])
            elif t == 4:
                state[r] = down(state[r])
            else:
                finish(rows, x1s[r], state[r])


def _mix_ffn(x2d, a, b, wo, g_post_mix, g_pre_ffn, g_post_ffn, w_gate, w_up, w_down):
    t = x2d.shape[0]
    tm = TM_FFN
    half = wo.shape[0] // 2
    tok = lambda n: pl.BlockSpec((tm, n), lambda i: (i, 0))
    vec = _const_spec((1, D_MODEL))
    return pl.pallas_call(
        _mix_ffn_kernel,
        grid=(t // tm,),
        in_specs=[tok(D_MODEL), tok(half), tok(half),
                  _const_spec((half, D_MODEL)), _const_spec((half, D_MODEL)),
                  vec, vec, vec,
                  _const_spec((D_MODEL, D_FF)), _const_spec((D_MODEL, D_FF)),
                  _const_spec((D_FF, D_MODEL))],
        out_specs=tok(D_MODEL),
        out_shape=jax.ShapeDtypeStruct((t, D_MODEL), F32),
        compiler_params=_params(("parallel",)),
        name="mix_out_ffn",
    )(x2d, a, b, wo[:half], wo[half:], g_post_mix, g_pre_ffn, g_post_ffn,
      w_gate, w_up, w_down)


def _l1_in_kernel(x_ref, g_ref, w_ref, wa_ref, wa2_ref, ba_ref,
                  q_ref, k_ref, v_ref, sg_ref, la_ref):
    sub = x_ref.shape[0] // IN_SUBTILES

    def project(r):
        rows = slice(r * sub, (r + 1) * sub)
        h = _rms_norm(x_ref[rows, :], g_ref[...]).astype(BF16)
        proj = jnp.dot(h, w_ref[...], preferred_element_type=F32)
        a_low = jnp.dot(h, wa_ref[...], preferred_element_type=F32)
        return proj, a_low

    def finish(r, proj, a_low):
        rows = slice(r * sub, (r + 1) * sub)
        z = (jnp.dot(a_low.astype(BF16), wa2_ref[...], preferred_element_type=F32)
             + ba_ref[...])
        log_sig = jnp.minimum(z, 0.0) - jnp.log1p(jnp.exp(-jnp.abs(z)))
        la_ref[rows, :] = log_sig * (1.0 / C_GATE_TAU)
        q_ref[rows, :] = proj[:, 0:C_KEY_DIM] * (C_DK ** -0.5)
        k_ref[rows, :] = proj[:, C_KEY_DIM:2 * C_KEY_DIM]
        v_ref[rows, :] = proj[:, 2 * C_KEY_DIM:2 * C_KEY_DIM + C_VAL_DIM].astype(BF16)
        gg = proj[:, 2 * C_KEY_DIM + C_VAL_DIM:C_MAIN]
        sg_ref[rows, :] = (gg * jax.nn.sigmoid(gg)).astype(BF16)

    cur = project(0)
    for r in range(IN_SUBTILES):
        upcoming = project(r + 1) if r + 1 < IN_SUBTILES else None
        finish(r, *cur)
        cur = upcoming


def _l1_in(x2d, g, w_main, w_a, w_a2, b_a):
    t = x2d.shape[0]
    tm = TM_IN1
    tok = lambda n: pl.BlockSpec((tm, n), lambda i: (i, 0))
    return pl.pallas_call(
        _l1_in_kernel,
        grid=(t // tm,),
        in_specs=[tok(D_MODEL), _const_spec((1, D_MODEL)), _const_spec((D_MODEL, C_MAIN)),
                  _const_spec((D_MODEL, C_GATE_RANK)), _const_spec((C_GATE_RANK, C_KEY_DIM)),
                  _const_spec((1, C_KEY_DIM))],
        out_specs=[tok(C_KEY_DIM), tok(C_KEY_DIM), tok(C_VAL_DIM), tok(C_VAL_DIM),
                   tok(C_KEY_DIM)],
        out_shape=[jax.ShapeDtypeStruct((t, C_KEY_DIM), F32),
                   jax.ShapeDtypeStruct((t, C_KEY_DIM), F32),
                   jax.ShapeDtypeStruct((t, C_VAL_DIM), BF16),
                   jax.ShapeDtypeStruct((t, C_VAL_DIM), BF16),
                   jax.ShapeDtypeStruct((t, C_KEY_DIM), F32)],
        compiler_params=_params(("parallel",)),
        name="l1_in_proj_gate",
    )(x2d, g, w_main, w_a, w_a2, b_a)


def _gla_levels(block):
    levels = []
    s = block // 2
    while s >= 1:
        levels.append(s)
        s //= 2
    return tuple(levels)


GLA_MXU_LEVELS = (4, 2)


def _range_sum_matrix(block):
    row = lax.broadcasted_iota(jnp.int32, (block, block), 0)
    col = lax.broadcasted_iota(jnp.int32, (block, block), 1)
    mats = [(col <= row).astype(F32)]
    for s in GLA_MXU_LEVELS:
        bnd = (row & (-2 * s)) + (s - 1)
        upper = (row & s) != 0
        first = jnp.where(upper, bnd, row)
        last = jnp.where(upper, row, bnd)
        mats.append(((col > first) & (col <= last)).astype(F32))
    return jnp.concatenate(mats, axis=0).astype(BF16)


def _level_factors(c_ref, c2, small_e, q, k, r0, ks, block, s):
    if s >= 8:
        parts = []
        for p in range(block // (2 * s)):
            lo = slice(2 * s * p, 2 * s * p + s)
            up = slice(2 * s * p + s, 2 * s * (p + 1))
            b = r0 + 2 * s * p + s - 1
            ref_row = c_ref[b:b + 1, ks]
            parts.append(k[lo] * jnp.exp2(ref_row - c2[lo]))
            parts.append(q[up] * jnp.exp2(c2[up] - ref_row))
        return jnp.concatenate(parts, axis=0)
    rowv = lax.broadcasted_iota(jnp.int32, (block, 1), 0)
    upper = (rowv & s) != 0
    return jnp.where(upper, q, k) * jnp.exp2(small_e[s])


def _gla_kernel(q_ref, k_ref, v_ref, sg_ref, la_ref, ng_ref, ya_ref, yb_ref,
                st_ref, c_ref, m_ref):
    blk = GLA_BLOCK

    @pl.when(pl.program_id(1) == 0)
    def _reset():
        st_ref[...] = jnp.zeros_like(st_ref)
        m_ref[...] = _range_sum_matrix(blk)

    levels = _gla_levels(blk)
    row = lax.broadcasted_iota(jnp.int32, (blk, blk), 0)
    col = lax.broadcasted_iota(jnp.int32, (blk, blk), 1)
    eye = col == row
    split = jnp.where(row > col, row ^ col, 0)
    odd = (lax.broadcasted_iota(jnp.int32, (blk, 1), 0) & 1) != 0
    nt = (((1,), (1,)), ((), ()))
    tn = (((0,), (0,)), ((), ()))
    n_blocks = q_ref.shape[0] // blk

    def range_sums(ci):
        rows = slice(ci * blk, (ci + 1) * blk)
        la2 = la_ref[rows, :] * LOG2E
        hi = la2.astype(BF16)
        mid = (la2 - hi.astype(F32)).astype(BF16)
        sums = (jnp.dot(m_ref[...], hi, preferred_element_type=F32)
                + jnp.dot(m_ref[...], mid, preferred_element_type=F32))
        c_ref[rows, :] = sums[0:blk]
        return la2, sums

    def factors(ci, h, la2, sums):
        r0 = ci * blk
        rows = slice(r0, r0 + blk)
        ks = slice(h * C_DK, (h + 1) * C_DK)
        q = q_ref[rows, ks]
        k = k_ref[rows, ks]
        c2 = sums[0:blk, ks]
        small_e = {1: jnp.where(odd, la2[:, ks], 0.0)}
        for n, s in enumerate(GLA_MXU_LEVELS):
            small_e[s] = sums[(n + 1) * blk:(n + 2) * blk, ks]
        zs = [_level_factors(c_ref, c2, small_e, q, k, r0, ks, blk, s).astype(BF16)
              for s in levels]
        c_last = c_ref[r0 + blk - 1:r0 + blk, ks]
        qe = (q * jnp.exp2(c2)).astype(BF16)
        kd = (k * jnp.exp2(c_last - c2)).astype(BF16)
        diag = jnp.sum(q * k, axis=-1, keepdims=True)
        return zs, qe, kd, diag, jnp.exp2(c_last)

    def mix(ci, h, zs, qe, kd, diag, state_decay):
        rows = slice(ci * blk, (ci + 1) * blk)
        vs = slice(h * C_DV, (h + 1) * C_DV)
        a = jnp.where(eye, diag, 0.0)
        for li in reversed(range(len(levels))):
            gram = lax.dot_general(zs[li], zs[li], nt, preferred_element_type=F32)
            a = jnp.where(split >= levels[li], gram, a)
        vh = v_ref[rows, vs]
        st = st_ref[h]
        o = (jnp.dot(a.astype(BF16), vh, preferred_element_type=F32)
             + lax.dot_general(qe, st.astype(BF16), nt, preferred_element_type=F32))
        st_ref[h] = (st * state_decay
                     + lax.dot_general(vh, kd, tn, preferred_element_type=F32))
        on = o * lax.rsqrt(jnp.mean(o * o, axis=-1, keepdims=True) + NORM_EPS)
        y = (on * ng_ref[:, vs] * sg_ref[rows, vs].astype(F32)).astype(BF16)
        if h < C_HEADS // 2:
            ya_ref[rows, h * C_DV:(h + 1) * C_DV] = y
        else:
            hh = h - C_HEADS // 2
            yb_ref[rows, hh * C_DV:(hh + 1) * C_DV] = y

    items = [(ci, h) for ci in range(n_blocks) for h in range(C_HEADS)]
    sums = {0: range_sums(0)}
    ready = factors(0, 0, *sums[0])
    for n, (ci, h) in enumerate(items):
        if h == 0 and ci + 1 < n_blocks:
            sums[ci + 1] = range_sums(ci + 1)
        upcoming = None
        if n + 1 < len(items):
            nci, nh = items[n + 1]
            upcoming = factors(nci, nh, *sums[nci])
        mix(ci, h, *ready)
        ready = upcoming


def _gla(q, k, v, sg, la, norm_g, batch):
    t = q.shape[0]
    tc = TC_GLA
    nt = t // batch // tc
    tok = lambda n: pl.BlockSpec((tc, n), lambda b, i: (b * nt + i, 0))
    half = C_VAL_DIM // 2
    out = jax.ShapeDtypeStruct((t, half), BF16)
    return pl.pallas_call(
        _gla_kernel,
        grid=(batch, nt),
        in_specs=[tok(C_KEY_DIM), tok(C_KEY_DIM), tok(C_VAL_DIM), tok(C_VAL_DIM),
                  tok(C_KEY_DIM), _const_spec((1, C_VAL_DIM))],
        out_specs=[tok(half), tok(half)],
        out_shape=[out, out],
        scratch_shapes=[pltpu.VMEM((C_HEADS, C_DV, C_DK), F32),
                        pltpu.VMEM((tc, C_KEY_DIM), F32),
                        pltpu.VMEM(((1 + len(GLA_MXU_LEVELS)) * GLA_BLOCK, GLA_BLOCK), BF16)],
        compiler_params=_params(("parallel", "arbitrary")),
        name="l1_gla_scan",
    )(q, k, v, sg, la, norm_g)


def _rel_bias_row(rel_bias):
    h = rel_bias.shape[0]
    last = rel_bias[:, 2 * A_MAX_REL:]
    head = jnp.broadcast_to(last, (h, A_MAX_REL))
    mid = rel_bias[:, ::-1]
    tail = jnp.broadcast_to(last, (h, BIAS_ROW - 3 * A_MAX_REL - 1))
    return jnp.concatenate([head, mid, tail], axis=1)


def kernel(x, pre_mix_g, post_mix_g, pre_ffn_g, post_ffn_g, ab_w_in, a_rel_bias, b_ln_g,
           b_ln_b, b_w_s, b_b_s, ab_w_out, c_w_in, c_w_a2, c_b_a, c_norm_g, c_w_out,
           ffn_w_gate, ffn_w_up, ffn_w_down):
    batch, seq, d = x.shape
    x2d = x.reshape(batch * seq, d)
    vec = lambda a: a.reshape(1, -1)
    bf = lambda a: a.astype(BF16)

    q, k, v, b_out = _l0_in(x2d, vec(pre_mix_g[0]), bf(ab_w_in[0]), vec(b_ln_g[0]),
                            vec(b_ln_b[0]), b_w_s[0], b_b_s[0].T)
    a_out = _attention(q, k, v, _rel_bias_row(a_rel_bias[0]), batch)
    x2d = _mix_ffn(x2d, a_out, b_out, bf(ab_w_out[0]), vec(post_mix_g[0]),
                   vec(pre_ffn_g[0]), vec(post_ffn_g[0]), bf(ffn_w_gate[0]),
                   bf(ffn_w_up[0]), bf(ffn_w_down[0]))

    w_in1 = c_w_in[0]
    q, k, v, sg, la = _l1_in(x2d, vec(pre_mix_g[1]), bf(w_in1[:, :C_MAIN]),
                             bf(w_in1[:, C_MAIN:]), bf(c_w_a2[0]), vec(c_b_a[0]))
    ya, yb = _gla(q, k, v, sg, la, vec(c_norm_g[0]), batch)
    x2d = _mix_ffn(x2d, ya, yb, bf(c_w_out[0]), vec(post_mix_g[1]), vec(pre_ffn_g[1]),
                   vec(post_ffn_g[1]), bf(ffn_w_gate[1]), bf(ffn_w_up[1]),
                   bf(ffn_w_down[1]))
    return x2d.reshape(batch, seq, d)
```

```python
import functools

import jax
import jax.numpy as jnp
from jax import lax
from jax.experimental import pallas as pl
from jax.experimental.pallas import tpu as pltpu

F32 = jnp.float32
BF16 = jnp.bfloat16

D_MODEL = 1024
CHUNK = 64
NORM_EPS = 1e-6
A_HEADS = 8
A_HEAD_DIM = 64
A_WIDTH = A_HEADS * A_HEAD_DIM
A_LEFT_CHUNKS = 8
A_MAX_REL = 256
B_GROUPS = 4
B_BLOCK = 128
B_WIDTH = D_MODEL // 2
AB_IN = 3 * A_WIDTH + 2 * B_WIDTH
C_HEADS = 4
C_KEY_DIM = D_MODEL // 2
C_VAL_DIM = D_MODEL
C_DK = C_KEY_DIM // C_HEADS
C_DV = C_VAL_DIM // C_HEADS
C_GATE_RANK = 16
C_GATE_TAU = 16.0
C_MAIN = 2 * C_KEY_DIM + 2 * C_VAL_DIM
C_IN = C_MAIN + C_GATE_RANK
D_FF = 2816

LANES = 128
C_IN_PAD = -(-C_IN // LANES) * LANES
VMEM_LIMIT_BYTES = 56 * 1024 * 1024

NEG_BIG = float(jnp.finfo(jnp.float32).min)
LOG2E = 1.4426950408889634

TM_IN0 = 1024
IN_SUBTILES = 4
TQ_ATTN = 256
TM_FFN = 1024
FFN_SUBTILES = 4
TM_IN1 = 1024
TC_GLA = 512
GLA_BLOCK = 128
ATTN_KEYS = TQ_ATTN + A_LEFT_CHUNKS * CHUNK
BIAS_ROW = 1024


def _rms_norm(x, g):
    ms = jnp.mean(x * x, axis=-1, keepdims=True)
    return x * lax.rsqrt(ms + NORM_EPS) * g


def _gelu(x):
    return 0.5 * x * (1.0 + lax.erf(x * (2.0 ** -0.5)))


def _const_spec(shape):
    nd = len(shape)
    return pl.BlockSpec(shape, lambda *_: (0,) * nd, pipeline_mode=pl.Buffered(1))


def _layer_spec(shape, layer, row_block=0):
    return pl.BlockSpec((None,) + tuple(shape), lambda *_: (layer, row_block, 0),
                        pipeline_mode=pl.Buffered(1))


def _params(sem):
    return pltpu.CompilerParams(dimension_semantics=sem,
                                vmem_limit_bytes=VMEM_LIMIT_BYTES)


def _l0_in_kernel(x_ref, g_ref, w_ref, lng_ref, lnb_ref, ws_ref, bst_ref,
                  q_ref, k_ref, v_ref, b_ref):
    sub = x_ref.shape[0] // IN_SUBTILES
    row = lax.broadcasted_iota(jnp.int32, (B_BLOCK, B_BLOCK), 0)
    col = lax.broadcasted_iota(jnp.int32, (B_BLOCK, B_BLOCK), 1)
    causal = col <= row
    w_gate = [jnp.where(causal, ws_ref[g], 0.0).astype(BF16) for g in range(B_GROUPS)]

    def project(r):
        rows = slice(r * sub, (r + 1) * sub)
        h = _rms_norm(x_ref[rows, :], g_ref[...]).astype(BF16)
        return jnp.dot(h, w_ref[...], preferred_element_type=F32)

    def finish(r, proj):
        rows = slice(r * sub, (r + 1) * sub)
        q_ref[rows, :] = (proj[:, 0:A_WIDTH] * (A_HEAD_DIM ** -0.5 * LOG2E)).astype(BF16)
        k_ref[rows, :] = proj[:, A_WIDTH:2 * A_WIDTH].astype(BF16)
        v_ref[rows, :] = proj[:, 2 * A_WIDTH:3 * A_WIDTH].astype(BF16)
        u = _gelu(proj[:, 3 * A_WIDTH:3 * A_WIDTH + B_WIDTH])
        zv = _gelu(proj[:, 3 * A_WIDTH + B_WIDTH:AB_IN])
        mu = jnp.mean(zv, axis=-1, keepdims=True)
        d = zv - mu
        var = jnp.mean(d * d, axis=-1, keepdims=True)
        vln = (d * lax.rsqrt(var + NORM_EPS) * lng_ref[...] + lnb_ref[...]).astype(BF16)
        for g in range(B_GROUPS):
            bcol = bst_ref[:, g:g + 1]
            cs = slice(g * LANES, (g + 1) * LANES)
            for n in range(sub // B_BLOCK):
                rs = slice(n * B_BLOCK, (n + 1) * B_BLOCK)
                f = jnp.dot(w_gate[g], vln[rs, cs], preferred_element_type=F32) + bcol
                out_rows = slice(r * sub + n * B_BLOCK, r * sub + (n + 1) * B_BLOCK)
                b_ref[out_rows, cs] = (u[rs, cs] * f).astype(BF16)

    proj = project(0)
    for r in range(IN_SUBTILES):
        upcoming = project(r + 1) if r + 1 < IN_SUBTILES else None
        finish(r, proj)
        proj = upcoming


def _l0_in(x2d, g, w_in, ln_g, ln_b, w_s, b_s_t):
    t = x2d.shape[0]
    tm = TM_IN0
    tok = lambda n: pl.BlockSpec((tm, n), lambda i: (i, 0))
    out = jax.ShapeDtypeStruct((t, A_WIDTH), BF16)
    return pl.pallas_call(
        _l0_in_kernel,
        grid=(t // tm,),
        in_specs=[tok(D_MODEL), _const_spec((1, D_MODEL)),
                  _layer_spec((D_MODEL, AB_IN), 0),
                  _const_spec((1, B_WIDTH)), _const_spec((1, B_WIDTH)),
                  _const_spec((B_GROUPS, B_BLOCK, B_BLOCK)), _const_spec((B_BLOCK, B_GROUPS))],
        out_specs=[tok(A_WIDTH)] * 4,
        out_shape=[out] * 4,
        compiler_params=_params(("parallel",)),
        name="l0_in_proj_gmlp",
    )(x2d, g, w_in, ln_g, ln_b, w_s, b_s_t)


def _attn_kernel(q_ref, k0_ref, k1_ref, k2_ref, v0_ref, v1_ref, v2_ref, rb_ref,
                 o_ref, bias_ref):
    i = pl.program_id(1)
    tq = TQ_ATTN
    nk = ATTN_KEYS

    @pl.when(i * tq <= A_LEFT_CHUNKS * CHUNK)
    def _build_bias():
        qq = lax.broadcasted_iota(jnp.int32, (tq, nk), 0)
        kk = lax.broadcasted_iota(jnp.int32, (tq, nk), 1)
        band_lo = (qq // CHUNK) * CHUNK
        keep = ((kk >= band_lo) & (kk < band_lo + (A_LEFT_CHUNKS + 1) * CHUNK)
                & (kk + (i * tq - A_LEFT_CHUNKS * CHUNK) >= 0))
        for h in range(A_HEADS):
            r = jnp.broadcast_to(rb_ref[h:h + 1, :], (tq, BIAS_ROW))
            t = pltpu.roll(r, 0, 1, stride=1, stride_axis=0)
            bias_ref[h] = jnp.where(keep, t[:, :nk] * LOG2E, NEG_BIG)

    kcat = jnp.concatenate([k0_ref[...], k1_ref[...], k2_ref[...]], axis=0)
    vcat = jnp.concatenate([v0_ref[...], v1_ref[...], v2_ref[...]], axis=0)
    lane = lax.broadcasted_iota(jnp.int32, (1, LANES), 1)
    lo = lane < A_HEAD_DIM
    nt = (((1,), (1,)), ((), ()))
    def head_sel(h):
        return lo if h % 2 == 0 else jnp.logical_not(lo)

    def scores(h):
        cs = slice((h // 2) * LANES, (h // 2 + 1) * LANES)
        qp = q_ref[:, cs]
        qh = jnp.where(head_sel(h), qp, jnp.zeros_like(qp))
        return lax.dot_general(qh, kcat[:, cs], nt, preferred_element_type=F32)

    def softmax_numerator(h, s):
        s = s + bias_ref[h]
        m = jnp.max(s, axis=-1, keepdims=True)
        return jnp.exp2(s - m).astype(BF16)

    def weighted_values(h, e):
        cs = slice((h // 2) * LANES, (h // 2 + 1) * LANES)
        ones_lane = A_HEAD_DIM if h % 2 == 0 else 0
        vh = jnp.where(head_sel(h), vcat[:, cs], (lane == ones_lane).astype(BF16))
        o = jnp.dot(e, vh, preferred_element_type=F32)
        return o / o[:, ones_lane:ones_lane + 1]

    s_val = {}
    e_val = {}
    outs = {}
    for t in range(A_HEADS + 2):
        if t < A_HEADS:
            s_val[t] = scores(t)
        if 0 <= t - 1 < A_HEADS:
            e_val[t - 1] = softmax_numerator(t - 1, s_val.pop(t - 1))
        h = t - 2
        if 0 <= h < A_HEADS:
            outs[h] = weighted_values(h, e_val.pop(h))
            if h % 2 == 1:
                cs = slice((h // 2) * LANES, (h // 2 + 1) * LANES)
                o_ref[:, cs] = jnp.where(lo, outs.pop(h - 1), outs.pop(h)).astype(BF16)


def _attention(q, k, v, rb_row, batch):
    t = q.shape[0]
    tq = TQ_ATTN
    nt = t // batch // tq
    cur = pl.BlockSpec((tq, A_WIDTH), lambda b, i: (b * nt + i, 0))
    prev1 = pl.BlockSpec((tq, A_WIDTH), lambda b, i: (b * nt + jnp.maximum(i - 1, 0), 0))
    prev2 = pl.BlockSpec((tq, A_WIDTH), lambda b, i: (b * nt + jnp.maximum(i - 2, 0), 0))
    return pl.pallas_call(
        _attn_kernel,
        grid=(batch, nt),
        in_specs=[cur, prev2, prev1, cur, prev2, prev1, cur,
                  _const_spec((A_HEADS, BIAS_ROW))],
        out_specs=cur,
        out_shape=jax.ShapeDtypeStruct((t, A_WIDTH), BF16),
        scratch_shapes=[pltpu.VMEM((A_HEADS, tq, ATTN_KEYS), F32)],
        compiler_params=_params(("parallel", "arbitrary")),
        name="l0_band_attention",
    )(q, k, k, k, v, v, v, rb_row)


def _mix_ffn_kernel(x_ref, a_ref, b_ref, woa_ref, wob_ref, gpm_ref, gpf_ref, gpo_ref,
                    wg_ref, wu_ref, wd_ref, o_ref):
    sub = x_ref.shape[0] // FFN_SUBTILES
    tiles = [slice(r * sub, (r + 1) * sub) for r in range(FFN_SUBTILES)]

    def mix_proj(rows):
        return (jnp.dot(a_ref[rows, :], woa_ref[...], preferred_element_type=F32)
                + jnp.dot(b_ref[rows, :], wob_ref[...], preferred_element_type=F32))

    def norms(rows, m):
        x1 = x_ref[rows, :] + _rms_norm(m, gpm_ref[...])
        return x1, _rms_norm(x1, gpf_ref[...]).astype(BF16)

    def gate_up(h):
        return (jnp.dot(h, wg_ref[...], preferred_element_type=F32),
                jnp.dot(h, wu_ref[...], preferred_element_type=F32))

    def activation(gu):
        gate, up = gu
        return (gate * jax.nn.sigmoid(gate) * up).astype(BF16)

    def down(act):
        return jnp.dot(act, wd_ref[...], preferred_element_type=F32)

    def finish(rows, x1, y):
        o_ref[rows, :] = x1 + _rms_norm(y, gpo_ref[...])

    stages = [mix_proj, norms, gate_up, activation, down, finish]
    state = [None] * FFN_SUBTILES
    x1s = [None] * FFN_SUBTILES
    for step in range(len(stages) + FFN_SUBTILES - 1):
        for r in range(FFN_SUBTILES):
            t = step - r
            if t < 0 or t >= len(stages):
                continue
            rows = tiles[r]
            if t == 0:
                state[r] = mix_proj(rows)
            elif t == 1:
                x1s[r], state[r] = norms(rows, state[r])
            elif t == 2:
                state[r] = gate_up(state[r])
            elif t == 3:
                state[r] = activation(state[r])
            elif t == 4:
                state[r] = down(state[r])
            else:
                finish(rows, x1s[r], state[r])


def _mix_ffn(x2d, a, b, wo, layer, g_post_mix, g_pre_ffn, g_post_ffn, w_gate, w_up, w_down):
    t = x2d.shape[0]
    tm = TM_FFN
    half = wo.shape[1] // 2
    tok = lambda n: pl.BlockSpec((tm, n), lambda i: (i, 0))
    vec = _layer_spec((1, D_MODEL), layer)
    return pl.pallas_call(
        _mix_ffn_kernel,
        grid=(t // tm,),
        in_specs=[tok(D_MODEL), tok(half), tok(half),
                  _layer_spec((half, D_MODEL), 0, 0), _layer_spec((half, D_MODEL), 0, 1),
                  vec, vec, vec,
                  _layer_spec((D_MODEL, D_FF), layer), _layer_spec((D_MODEL, D_FF), layer),
                  _layer_spec((D_FF, D_MODEL), layer)],
        out_specs=tok(D_MODEL),
        out_shape=jax.ShapeDtypeStruct((t, D_MODEL), F32),
        compiler_params=_params(("parallel",)),
        name="mix_out_ffn",
    )(x2d, a, b, wo, wo, g_post_mix, g_pre_ffn, g_post_ffn, w_gate, w_up, w_down)


def _l1_in_kernel(x_ref, g_ref, w_ref, wa2_ref, ba_ref,
                  q_ref, k_ref, v_ref, sg_ref, la_ref):
    sub = x_ref.shape[0] // IN_SUBTILES

    def project(r):
        rows = slice(r * sub, (r + 1) * sub)
        h = _rms_norm(x_ref[rows, :], g_ref[...]).astype(BF16)
        return jnp.dot(h, w_ref[...], preferred_element_type=F32)

    def finish(r, proj):
        rows = slice(r * sub, (r + 1) * sub)
        a_low = proj[:, C_MAIN:C_MAIN + C_GATE_RANK]
        z = (jnp.dot(a_low.astype(BF16), wa2_ref[...], preferred_element_type=F32)
             + ba_ref[...])
        log_sig = jnp.minimum(z, 0.0) - jnp.log1p(jnp.exp(-jnp.abs(z)))
        la_ref[rows, :] = log_sig * (1.0 / C_GATE_TAU)
        q_ref[rows, :] = (proj[:, 0:C_KEY_DIM] * (C_DK ** -0.5)).astype(BF16)
        k_ref[rows, :] = proj[:, C_KEY_DIM:2 * C_KEY_DIM].astype(BF16)
        v_ref[rows, :] = proj[:, 2 * C_KEY_DIM:2 * C_KEY_DIM + C_VAL_DIM].astype(BF16)
        gg = proj[:, 2 * C_KEY_DIM + C_VAL_DIM:C_MAIN]
        sg_ref[rows, :] = (gg * jax.nn.sigmoid(gg)).astype(BF16)

    cur = project(0)
    for r in range(IN_SUBTILES):
        upcoming = project(r + 1) if r + 1 < IN_SUBTILES else None
        finish(r, cur)
        cur = upcoming


def _l1_in(x2d, g, w_in, w_a2, b_a):
    t = x2d.shape[0]
    tm = TM_IN1
    tok = lambda n: pl.BlockSpec((tm, n), lambda i: (i, 0))
    return pl.pallas_call(
        _l1_in_kernel,
        grid=(t // tm,),
        in_specs=[tok(D_MODEL), _const_spec((1, D_MODEL)), _const_spec((D_MODEL, C_IN_PAD)),
                  _const_spec((C_GATE_RANK, C_KEY_DIM)), _const_spec((1, C_KEY_DIM))],
        out_specs=[tok(C_KEY_DIM), tok(C_KEY_DIM), tok(C_VAL_DIM), tok(C_VAL_DIM),
                   tok(C_KEY_DIM)],
        out_shape=[jax.ShapeDtypeStruct((t, C_KEY_DIM), BF16),
                   jax.ShapeDtypeStruct((t, C_KEY_DIM), BF16),
                   jax.ShapeDtypeStruct((t, C_VAL_DIM), BF16),
                   jax.ShapeDtypeStruct((t, C_VAL_DIM), BF16),
                   jax.ShapeDtypeStruct((t, C_KEY_DIM), F32)],
        compiler_params=_params(("parallel",)),
        name="l1_in_proj_gate",
    )(x2d, g, w_in, w_a2, b_a)


def _gla_levels(block):
    levels = []
    s = block // 2
    while s >= 1:
        levels.append(s)
        s //= 2
    return tuple(levels)


GLA_MXU_LEVELS = (4, 2)


def _range_sum_matrix(block):
    row = lax.broadcasted_iota(jnp.int32, (block, block), 0)
    col = lax.broadcasted_iota(jnp.int32, (block, block), 1)
    mats = [(col <= row).astype(F32)]
    for s in GLA_MXU_LEVELS:
        bnd = (row & (-2 * s)) + (s - 1)
        upper = (row & s) != 0
        first = jnp.where(upper, bnd, row)
        last = jnp.where(upper, row, bnd)
        mats.append(((col > first) & (col <= last)).astype(F32))
    return jnp.concatenate(mats, axis=0).astype(BF16)


def _level_factors(c_ref, c2, small_e, q, k, r0, ks, block, s):
    if s >= 8:
        parts = []
        for p in range(block // (2 * s)):
            lo = slice(2 * s * p, 2 * s * p + s)
            up = slice(2 * s * p + s, 2 * s * (p + 1))
            b = r0 + 2 * s * p + s - 1
            ref_row = c_ref[b:b + 1, ks]
            parts.append(k[lo] * jnp.exp2(ref_row - c2[lo]))
            parts.append(q[up] * jnp.exp2(c2[up] - ref_row))
        return jnp.concatenate(parts, axis=0)
    rowv = lax.broadcasted_iota(jnp.int32, (block, 1), 0)
    upper = (rowv & s) != 0
    return jnp.where(upper, q, k) * jnp.exp2(small_e[s])


def _gla_kernel(q_ref, k_ref, v_ref, sg_ref, la_ref, ng_ref, ya_ref, yb_ref,
                st_ref, c_ref, m_ref):
    blk = GLA_BLOCK

    @pl.when(pl.program_id(1) == 0)
    def _reset():
        st_ref[...] = jnp.zeros_like(st_ref)
        m_ref[...] = _range_sum_matrix(blk)

    levels = _gla_levels(blk)
    row = lax.broadcasted_iota(jnp.int32, (blk, blk), 0)
    col = lax.broadcasted_iota(jnp.int32, (blk, blk), 1)
    eye = col == row
    split = jnp.where(row > col, row ^ col, 0)
    odd = (lax.broadcasted_iota(jnp.int32, (blk, 1), 0) & 1) != 0
    nt = (((1,), (1,)), ((), ()))
    tn = (((0,), (0,)), ((), ()))
    n_blocks = q_ref.shape[0] // blk

    def range_sums(ci):
        rows = slice(ci * blk, (ci + 1) * blk)
        la2 = la_ref[rows, :] * LOG2E
        hi = la2.astype(BF16)
        mid = (la2 - hi.astype(F32)).astype(BF16)
        sums = (jnp.dot(m_ref[...], hi, preferred_element_type=F32)
                + jnp.dot(m_ref[...], mid, preferred_element_type=F32))
        c_ref[rows, :] = sums[0:blk]
        return la2, sums

    def factors(ci, h, la2, sums):
        r0 = ci * blk
        rows = slice(r0, r0 + blk)
        ks = slice(h * C_DK, (h + 1) * C_DK)
        q = q_ref[rows, ks].astype(F32)
        k = k_ref[rows, ks].astype(F32)
        c2 = sums[0:blk, ks]
        small_e = {1: jnp.where(odd, la2[:, ks], 0.0)}
        for n, s in enumerate(GLA_MXU_LEVELS):
            small_e[s] = sums[(n + 1) * blk:(n + 2) * blk, ks]
        zs = [_level_factors(c_ref, c2, small_e, q, k, r0, ks, blk, s).astype(BF16)
              for s in levels]
        c_last = c_ref[r0 + blk - 1:r0 + blk, ks]
        qe = (q * jnp.exp2(c2)).astype(BF16)
        kd = (k * jnp.exp2(c_last - c2)).astype(BF16)
        diag = jnp.sum(q * k, axis=-1, keepdims=True)
        return zs, qe, kd, diag, jnp.exp2(c_last)

    def mix(ci, h, zs, qe, kd, diag, state_decay):
        rows = slice(ci * blk, (ci + 1) * blk)
        vs = slice(h * C_DV, (h + 1) * C_DV)
        a = jnp.where(eye, diag, 0.0)
        for li in reversed(range(len(levels))):
            gram = lax.dot_general(zs[li], zs[li], nt, preferred_element_type=F32)
            a = jnp.where(split >= levels[li], gram, a)
        vh = v_ref[rows, vs]
        st = st_ref[h]
        o = (jnp.dot(a.astype(BF16), vh, preferred_element_type=F32)
             + lax.dot_general(qe, st.astype(BF16), nt, preferred_element_type=F32))
        st_ref[h] = (st * state_decay
                     + lax.dot_general(vh, kd, tn, preferred_element_type=F32))
        on = o * lax.rsqrt(jnp.mean(o * o, axis=-1, keepdims=True) + NORM_EPS)
        y = (on * ng_ref[:, vs] * sg_ref[rows, vs].astype(F32)).astype(BF16)
        if h < C_HEADS // 2:
            ya_ref[rows, h * C_DV:(h + 1) * C_DV] = y
        else:
            hh = h - C_HEADS // 2
            yb_ref[rows, hh * C_DV:(hh + 1) * C_DV] = y

    items = [(ci, h) for ci in range(n_blocks) for h in range(C_HEADS)]
    sums = {0: range_sums(0)}
    ready = factors(0, 0, *sums[0])
    for n, (ci, h) in enumerate(items):
        if h == 0 and ci + 1 < n_blocks:
            sums[ci + 1] = range_sums(ci + 1)
        upcoming = None
        if n + 1 < len(items):
            nci, nh = items[n + 1]
            upcoming = factors(nci, nh, *sums[nci])
        mix(ci, h, *ready)
        ready = upcoming


def _gla(q, k, v, sg, la, norm_g, batch):
    t = q.shape[0]
    tc = TC_GLA
    nt = t // batch // tc
    tok = lambda n: pl.BlockSpec((tc, n), lambda b, i: (b * nt + i, 0))
    half = C_VAL_DIM // 2
    out = jax.ShapeDtypeStruct((t, half), BF16)
    return pl.pallas_call(
        _gla_kernel,
        grid=(batch, nt),
        in_specs=[tok(C_KEY_DIM), tok(C_KEY_DIM), tok(C_VAL_DIM), tok(C_VAL_DIM),
                  tok(C_KEY_DIM), _const_spec((1, C_VAL_DIM))],
        out_specs=[tok(half), tok(half)],
        out_shape=[out, out],
        scratch_shapes=[pltpu.VMEM((C_HEADS, C_DV, C_DK), F32),
                        pltpu.VMEM((tc, C_KEY_DIM), F32),
                        pltpu.VMEM(((1 + len(GLA_MXU_LEVELS)) * GLA_BLOCK, GLA_BLOCK), BF16)],
        compiler_params=_params(("parallel", "arbitrary")),
        name="l1_gla_scan",
    )(q, k, v, sg, la, norm_g)


def _rel_bias_row(rel_bias):
    h = rel_bias.shape[0]
    last = rel_bias[:, 2 * A_MAX_REL:]
    head = jnp.broadcast_to(last, (h, A_MAX_REL))
    mid = rel_bias[:, ::-1]
    tail = jnp.broadcast_to(last, (h, BIAS_ROW - 3 * A_MAX_REL - 1))
    return jnp.concatenate([head, mid, tail], axis=1)


def kernel(x, pre_mix_g, post_mix_g, pre_ffn_g, post_ffn_g, ab_w_in, a_rel_bias, b_ln_g,
           b_ln_b, b_w_s, b_b_s, ab_w_out, c_w_in, c_w_a2, c_b_a, c_norm_g, c_w_out,
           ffn_w_gate, ffn_w_up, ffn_w_down):
    batch, seq, d = x.shape
    x2d = x.reshape(batch * seq, d)
    vec = lambda a: a.reshape(1, -1)
    bf = lambda a: a.astype(BF16)
    depth = pre_mix_g.shape[0]
    gains = [g.reshape(depth, 1, d) for g in (post_mix_g, pre_ffn_g, post_ffn_g)]
    ffn_w = [bf(ffn_w_gate), bf(ffn_w_up), bf(ffn_w_down)]

    q, k, v, b_out = _l0_in(x2d, vec(pre_mix_g[0]), bf(ab_w_in), vec(b_ln_g[0]),
                            vec(b_ln_b[0]), b_w_s[0], b_b_s[0].T)
    a_out = _attention(q, k, v, _rel_bias_row(a_rel_bias[0]), batch)
    x2d = _mix_ffn(x2d, a_out, b_out, bf(ab_w_out), 0, *gains, *ffn_w)

    w_in1 = bf(jnp.pad(c_w_in[0], ((0, 0), (0, C_IN_PAD - C_IN))))
    q, k, v, sg, la = _l1_in(x2d, vec(pre_mix_g[1]), w_in1, bf(c_w_a2[0]), vec(c_b_a[0]))
    ya, yb = _gla(q, k, v, sg, la, vec(c_norm_g[0]), batch)
    x2d = _mix_ffn(x2d, ya, yb, bf(c_w_out), 1, *gains, *ffn_w)
    return x2d.reshape(batch, seq, d)
```

```python
import functools

import jax
import jax.numpy as jnp
from jax import lax
from jax.experimental import pallas as pl
from jax.experimental.pallas import tpu as pltpu

F32 = jnp.float32
BF16 = jnp.bfloat16

D_MODEL = 1024
CHUNK = 64
NORM_EPS = 1e-6
A_HEADS = 8
A_HEAD_DIM = 64
A_WIDTH = A_HEADS * A_HEAD_DIM
A_LEFT_CHUNKS = 8
A_MAX_REL = 256
B_GROUPS = 4
B_BLOCK = 128
B_WIDTH = D_MODEL // 2
AB_IN = 3 * A_WIDTH + 2 * B_WIDTH
C_HEADS = 4
C_KEY_DIM = D_MODEL // 2
C_VAL_DIM = D_MODEL
C_DK = C_KEY_DIM // C_HEADS
C_DV = C_VAL_DIM // C_HEADS
C_GATE_RANK = 16
C_GATE_TAU = 16.0
C_MAIN = 2 * C_KEY_DIM + 2 * C_VAL_DIM
C_IN = C_MAIN + C_GATE_RANK
D_FF = 2816

LANES = 128
C_IN_PAD = -(-C_IN // LANES) * LANES
VMEM_LIMIT_BYTES = 56 * 1024 * 1024

NEG_BIG = float(jnp.finfo(jnp.float32).min)
LOG2E = 1.4426950408889634

TM_IN0 = 1024
IN_SUBTILES = 4
TQ_ATTN = 256
ATTN_TILES = 2
TM_FFN = 1024
FFN_SUBTILES = 4
TM_IN1 = 1024
TC_GLA = 1024
GLA_BLOCK = 128
ATTN_KEYS = TQ_ATTN + A_LEFT_CHUNKS * CHUNK
BIAS_ROW = 1024


def _rms_norm(x, g):
    ms = jnp.mean(x * x, axis=-1, keepdims=True)
    return x * lax.rsqrt(ms + NORM_EPS) * g


def _gelu(x):
    return 0.5 * x * (1.0 + lax.erf(x * (2.0 ** -0.5)))


def _const_spec(shape):
    nd = len(shape)
    return pl.BlockSpec(shape, lambda *_: (0,) * nd, pipeline_mode=pl.Buffered(1))


def _layer_spec(shape, layer, row_block=0):
    return pl.BlockSpec((None,) + tuple(shape), lambda *_: (layer, row_block, 0),
                        pipeline_mode=pl.Buffered(1))


def _params(sem):
    return pltpu.CompilerParams(dimension_semantics=sem,
                                vmem_limit_bytes=VMEM_LIMIT_BYTES)


def _l0_in_kernel(x_ref, g_ref, w_ref, lng_ref, lnb_ref, ws_ref, bst_ref,
                  q_ref, k_ref, v_ref, b_ref):
    sub = x_ref.shape[0] // IN_SUBTILES
    row = lax.broadcasted_iota(jnp.int32, (B_BLOCK, B_BLOCK), 0)
    col = lax.broadcasted_iota(jnp.int32, (B_BLOCK, B_BLOCK), 1)
    causal = col <= row
    w_gate = [jnp.where(causal, ws_ref[g], 0.0).astype(BF16) for g in range(B_GROUPS)]

    n_attn = 3 * A_WIDTH
    hidden = {}

    def project(r, part):
        rows = slice(r * sub, (r + 1) * sub)
        if r not in hidden:
            hidden[r] = _rms_norm(x_ref[rows, :], g_ref[...]).astype(BF16)
        cols = slice(n_attn, AB_IN) if part == "gmlp" else slice(0, n_attn)
        return jnp.dot(hidden[r], w_ref[:, cols], preferred_element_type=F32)

    def finish_attn(r, proj):
        rows = slice(r * sub, (r + 1) * sub)
        q_ref[rows, :] = (proj[:, 0:A_WIDTH] * (A_HEAD_DIM ** -0.5 * LOG2E)).astype(BF16)
        k_ref[rows, :] = proj[:, A_WIDTH:2 * A_WIDTH].astype(BF16)
        v_ref[rows, :] = proj[:, 2 * A_WIDTH:3 * A_WIDTH].astype(BF16)

    def finish_gmlp(r, proj):
        u = _gelu(proj[:, 0:B_WIDTH])
        zv = _gelu(proj[:, B_WIDTH:2 * B_WIDTH])
        mu = jnp.mean(zv, axis=-1, keepdims=True)
        d = zv - mu
        var = jnp.mean(d * d, axis=-1, keepdims=True)
        vln = (d * lax.rsqrt(var + NORM_EPS) * lng_ref[...] + lnb_ref[...]).astype(BF16)
        for g in range(B_GROUPS):
            bcol = bst_ref[:, g:g + 1]
            cs = slice(g * LANES, (g + 1) * LANES)
            for n in range(sub // B_BLOCK):
                rs = slice(n * B_BLOCK, (n + 1) * B_BLOCK)
                f = jnp.dot(w_gate[g], vln[rs, cs], preferred_element_type=F32) + bcol
                out_rows = slice(r * sub + n * B_BLOCK, r * sub + (n + 1) * B_BLOCK)
                b_ref[out_rows, cs] = (u[rs, cs] * f).astype(BF16)

    items = [(r, part) for r in range(IN_SUBTILES) for part in ("gmlp", "attn")]
    proj = project(*items[0])
    for n, (r, part) in enumerate(items):
        upcoming = project(*items[n + 1]) if n + 1 < len(items) else None
        (finish_gmlp if part == "gmlp" else finish_attn)(r, proj)
        proj = upcoming


def _l0_in(x2d, g, w_in, ln_g, ln_b, w_s, b_s_t):
    t = x2d.shape[0]
    tm = TM_IN0
    tok = lambda n: pl.BlockSpec((tm, n), lambda i: (i, 0))
    out = jax.ShapeDtypeStruct((t, A_WIDTH), BF16)
    return pl.pallas_call(
        _l0_in_kernel,
        grid=(t // tm,),
        in_specs=[tok(D_MODEL), _const_spec((1, D_MODEL)),
                  _layer_spec((D_MODEL, AB_IN), 0),
                  _const_spec((1, B_WIDTH)), _const_spec((1, B_WIDTH)),
                  _const_spec((B_GROUPS, B_BLOCK, B_BLOCK)), _const_spec((B_BLOCK, B_GROUPS))],
        out_specs=[tok(A_WIDTH)] * 4,
        out_shape=[out] * 4,
        compiler_params=_params(("parallel",)),
        name="l0_in_proj_gmlp",
    )(x2d, g, w_in, ln_g, ln_b, w_s, b_s_t)


def _attn_kernel(q_ref, kp_ref, kc_ref, vp_ref, vc_ref, rb_ref, o_ref, bias_ref):
    i = pl.program_id(1)
    tq = TQ_ATTN
    nk = ATTN_KEYS
    step = ATTN_TILES * tq
    left = A_LEFT_CHUNKS * CHUNK

    @pl.when((i - 1) * step < left)
    def _build_bias():
        qq = lax.broadcasted_iota(jnp.int32, (tq, nk), 0)
        kk = lax.broadcasted_iota(jnp.int32, (tq, nk), 1)
        band_lo = (qq // CHUNK) * CHUNK
        in_band = (kk >= band_lo) & (kk < band_lo + (A_LEFT_CHUNKS + 1) * CHUNK)
        for h in range(A_HEADS):
            r = jnp.broadcast_to(rb_ref[h:h + 1, :], (tq, BIAS_ROW))
            t = pltpu.roll(r, 0, 1, stride=1, stride_axis=0)[:, :nk] * LOG2E
            for tile in range(ATTN_TILES):
                first_key = i * step + tile * tq - left
                bias_ref[tile, h] = jnp.where(in_band & (kk + first_key >= 0), t, NEG_BIG)

    kcat = jnp.concatenate([kp_ref[...], kc_ref[...]], axis=0)
    vcat = jnp.concatenate([vp_ref[...], vc_ref[...]], axis=0)
    lane = lax.broadcasted_iota(jnp.int32, (1, LANES), 1)
    lo = lane < A_HEAD_DIM
    nt = (((1,), (1,)), ((), ()))

    def head_sel(h):
        return lo if h % 2 == 0 else jnp.logical_not(lo)

    def scores(tile, h):
        cs = slice((h // 2) * LANES, (h // 2 + 1) * LANES)
        qp = q_ref[tile * tq:(tile + 1) * tq, cs]
        qh = jnp.where(head_sel(h), qp, jnp.zeros_like(qp))
        return lax.dot_general(qh, kcat[tile * tq:tile * tq + nk, cs], nt,
                               preferred_element_type=F32)

    def softmax_numerator(tile, h, s):
        s = s + bias_ref[tile, h]
        m = jnp.max(s, axis=-1, keepdims=True)
        return jnp.exp2(s - m).astype(BF16)

    def weighted_values(tile, h, e):
        cs = slice((h // 2) * LANES, (h // 2 + 1) * LANES)
        ones_lane = A_HEAD_DIM if h % 2 == 0 else 0
        vh = jnp.where(head_sel(h), vcat[tile * tq:tile * tq + nk, cs],
                       (lane == ones_lane).astype(BF16))
        o = jnp.dot(e, vh, preferred_element_type=F32)
        return o / o[:, ones_lane:ones_lane + 1]

    items = [(tile, h) for tile in range(ATTN_TILES) for h in range(A_HEADS)]
    s_val = {}
    e_val = {}
    outs = {}
    for t in range(len(items) + 2):
        if t < len(items):
            s_val[t] = scores(*items[t])
        if 0 <= t - 1 < len(items):
            e_val[t - 1] = softmax_numerator(*items[t - 1], s_val.pop(t - 1))
        n = t - 2
        if 0 <= n < len(items):
            tile, h = items[n]
            outs[h] = weighted_values(tile, h, e_val.pop(n))
            if h % 2 == 1:
                cs = slice((h // 2) * LANES, (h // 2 + 1) * LANES)
                o_ref[tile * tq:(tile + 1) * tq, cs] = jnp.where(
                    lo, outs.pop(h - 1), outs.pop(h)).astype(BF16)


def _attention(q, k, v, rb_row, batch):
    t = q.shape[0]
    step = ATTN_TILES * TQ_ATTN
    assert step == A_LEFT_CHUNKS * CHUNK
    nt = t // batch // step
    cur = pl.BlockSpec((step, A_WIDTH), lambda b, i: (b * nt + i, 0))
    prev = pl.BlockSpec((step, A_WIDTH), lambda b, i: (b * nt + jnp.maximum(i - 1, 0), 0))
    return pl.pallas_call(
        _attn_kernel,
        grid=(batch, nt),
        in_specs=[cur, prev, cur, prev, cur, _const_spec((A_HEADS, BIAS_ROW))],
        out_specs=cur,
        out_shape=jax.ShapeDtypeStruct((t, A_WIDTH), BF16),
        scratch_shapes=[pltpu.VMEM((ATTN_TILES, A_HEADS, TQ_ATTN, ATTN_KEYS), F32)],
        compiler_params=_params(("parallel", "arbitrary")),
        name="l0_band_attention",
    )(q, k, k, v, v, rb_row)


def _mix_ffn_kernel(x_ref, a_ref, b_ref, woa_ref, wob_ref, gpm_ref, gpf_ref, gpo_ref,
                    wg_ref, wu_ref, wd_ref, o_ref):
    sub = x_ref.shape[0] // FFN_SUBTILES
    tiles = [slice(r * sub, (r + 1) * sub) for r in range(FFN_SUBTILES)]

    def mix_proj(rows):
        return (jnp.dot(a_ref[rows, :], woa_ref[...], preferred_element_type=F32)
                + jnp.dot(b_ref[rows, :], wob_ref[...], preferred_element_type=F32))

    def norms(rows, m):
        x1 = x_ref[rows, :] + _rms_norm(m, gpm_ref[...])
        return x1, _rms_norm(x1, gpf_ref[...]).astype(BF16)

    def gate_up(h):
        return (jnp.dot(h, wg_ref[...], preferred_element_type=F32),
                jnp.dot(h, wu_ref[...], preferred_element_type=F32))

    def activation(gu):
        gate, up = gu
        return (gate * jax.nn.sigmoid(gate) * up).astype(BF16)

    def down(act):
        return jnp.dot(act, wd_ref[...], preferred_element_type=F32)

    def finish(rows, x1, y):
        o_ref[rows, :] = x1 + _rms_norm(y, gpo_ref[...])

    stages = [mix_proj, norms, gate_up, activation, down, finish]
    state = [None] * FFN_SUBTILES
    x1s = [None] * FFN_SUBTILES
    for step in range(len(stages) + FFN_SUBTILES - 1):
        for r in range(FFN_SUBTILES):
            t = step - r
            if t < 0 or t >= len(stages):
                continue
            rows = tiles[r]
            if t == 0:
                state[r] = mix_proj(rows)
            elif t == 1:
                x1s[r], state[r] = norms(rows, state[r])
            elif t == 2:
                state[r] = gate_up(state[r])
            elif t == 3:
                state[r] = activation(state[r])
            elif t == 4:
                state[r] = down(state[r])
            else:
                finish(rows, x1s[r], state[r])


def _mix_ffn(x2d, a, b, wo, layer, g_post_mix, g_pre_ffn, g_post_ffn, w_gate, w_up, w_down):
    t = x2d.shape[0]
    tm = TM_FFN
    half = wo.shape[1] // 2
    tok = lambda n: pl.BlockSpec((tm, n), lambda i: (i, 0))
    vec = _layer_spec((1, D_MODEL), layer)
    return pl.pallas_call(
        _mix_ffn_kernel,
        grid=(t // tm,),
        in_specs=[tok(D_MODEL), tok(half), tok(half),
                  _layer_spec((half, D_MODEL), 0, 0), _layer_spec((half, D_MODEL), 0, 1),
                  vec, vec, vec,
                  _layer_spec((D_MODEL, D_FF), layer), _layer_spec((D_MODEL, D_FF), layer),
                  _layer_spec((D_FF, D_MODEL), layer)],
        out_specs=tok(D_MODEL),
        out_shape=jax.ShapeDtypeStruct((t, D_MODEL), F32),
        compiler_params=_params(("parallel",)),
        name="mix_out_ffn",
    )(x2d, a, b, wo, wo, g_post_mix, g_pre_ffn, g_post_ffn, w_gate, w_up, w_down)


def _l1_in_kernel(x_ref, g_ref, w_ref, wa2_ref, ba_ref,
                  q_ref, k_ref, v_ref, sg_ref, la_ref):
    sub = x_ref.shape[0] // IN_SUBTILES

    n_qkv = 2 * C_KEY_DIM + C_VAL_DIM
    hidden = {}

    def project(r, part):
        rows = slice(r * sub, (r + 1) * sub)
        if r not in hidden:
            hidden[r] = _rms_norm(x_ref[rows, :], g_ref[...]).astype(BF16)
        cols = slice(n_qkv, C_IN_PAD) if part == "gates" else slice(0, n_qkv)
        return jnp.dot(hidden[r], w_ref[:, cols], preferred_element_type=F32)

    def finish_qkv(r, proj):
        rows = slice(r * sub, (r + 1) * sub)
        q_ref[rows, :] = (proj[:, 0:C_KEY_DIM] * (C_DK ** -0.5)).astype(BF16)
        k_ref[rows, :] = proj[:, C_KEY_DIM:2 * C_KEY_DIM].astype(BF16)
        v_ref[rows, :] = proj[:, 2 * C_KEY_DIM:n_qkv].astype(BF16)

    def finish_gates(r, proj):
        rows = slice(r * sub, (r + 1) * sub)
        a_low = proj[:, C_VAL_DIM:C_VAL_DIM + C_GATE_RANK]
        z = (jnp.dot(a_low.astype(BF16), wa2_ref[...], preferred_element_type=F32)
             + ba_ref[...])
        log_sig = jnp.minimum(z, 0.0) - jnp.log1p(jnp.exp(-jnp.abs(z)))
        la_ref[rows, :] = log_sig * (1.0 / C_GATE_TAU)
        gg = proj[:, 0:C_VAL_DIM]
        sg_ref[rows, :] = (gg * jax.nn.sigmoid(gg)).astype(BF16)

    items = [(r, part) for r in range(IN_SUBTILES) for part in ("gates", "qkv")]
    proj = project(*items[0])
    for n, (r, part) in enumerate(items):
        upcoming = project(*items[n + 1]) if n + 1 < len(items) else None
        (finish_gates if part == "gates" else finish_qkv)(r, proj)
        proj = upcoming


def _l1_in(x2d, g, w_in, w_a2, b_a):
    t = x2d.shape[0]
    tm = TM_IN1
    tok = lambda n: pl.BlockSpec((tm, n), lambda i: (i, 0))
    return pl.pallas_call(
        _l1_in_kernel,
        grid=(t // tm,),
        in_specs=[tok(D_MODEL), _const_spec((1, D_MODEL)), _const_spec((D_MODEL, C_IN_PAD)),
                  _const_spec((C_GATE_RANK, C_KEY_DIM)), _const_spec((1, C_KEY_DIM))],
        out_specs=[tok(C_KEY_DIM), tok(C_KEY_DIM), tok(C_VAL_DIM), tok(C_VAL_DIM),
                   tok(C_KEY_DIM)],
        out_shape=[jax.ShapeDtypeStruct((t, C_KEY_DIM), BF16),
                   jax.ShapeDtypeStruct((t, C_KEY_DIM), BF16),
                   jax.ShapeDtypeStruct((t, C_VAL_DIM), BF16),
                   jax.ShapeDtypeStruct((t, C_VAL_DIM), BF16),
                   jax.ShapeDtypeStruct((t, C_KEY_DIM), F32)],
        compiler_params=_params(("parallel",)),
        name="l1_in_proj_gate",
    )(x2d, g, w_in, w_a2, b_a)


def _gla_levels(block):
    levels = []
    s = block // 2
    while s >= 1:
        levels.append(s)
        s //= 2
    return tuple(levels)


GLA_MXU_LEVELS = (4, 2)


def _range_sum_matrix(block):
    row = lax.broadcasted_iota(jnp.int32, (block, block), 0)
    col = lax.broadcasted_iota(jnp.int32, (block, block), 1)
    mats = [(col <= row).astype(F32)]
    for s in GLA_MXU_LEVELS:
        bnd = (row & (-2 * s)) + (s - 1)
        upper = (row & s) != 0
        first = jnp.where(upper, bnd, row)
        last = jnp.where(upper, row, bnd)
        mats.append(((col > first) & (col <= last)).astype(F32))
    return jnp.concatenate(mats, axis=0).astype(BF16)


def _level_factors(c_ref, c2, small_e, q, k, r0, ks, block, s):
    if s >= 8:
        parts = []
        for p in range(block // (2 * s)):
            lo = slice(2 * s * p, 2 * s * p + s)
            up = slice(2 * s * p + s, 2 * s * (p + 1))
            b = r0 + 2 * s * p + s - 1
            ref_row = c_ref[b:b + 1, ks]
            parts.append(k[lo] * jnp.exp2(ref_row - c2[lo]))
            parts.append(q[up] * jnp.exp2(c2[up] - ref_row))
        return jnp.concatenate(parts, axis=0)
    rowv = lax.broadcasted_iota(jnp.int32, (block, 1), 0)
    upper = (rowv & s) != 0
    return jnp.where(upper, q, k) * jnp.exp2(small_e[s])


def _gla_kernel(q_ref, k_ref, v_ref, sg_ref, la_ref, ng_ref, ya_ref, yb_ref,
                st_ref, c_ref, m_ref):
    blk = GLA_BLOCK

    @pl.when(pl.program_id(1) == 0)
    def _reset():
        st_ref[...] = jnp.zeros_like(st_ref)
        m_ref[...] = _range_sum_matrix(blk)

    levels = _gla_levels(blk)
    row = lax.broadcasted_iota(jnp.int32, (blk, blk), 0)
    col = lax.broadcasted_iota(jnp.int32, (blk, blk), 1)
    eye = col == row
    split = jnp.where(row > col, row ^ col, 0)
    odd = (lax.broadcasted_iota(jnp.int32, (blk, 1), 0) & 1) != 0
    nt = (((1,), (1,)), ((), ()))
    tn = (((0,), (0,)), ((), ()))
    n_blocks = q_ref.shape[0] // blk

    def range_sums(ci):
        rows = slice(ci * blk, (ci + 1) * blk)
        la2 = la_ref[rows, :] * LOG2E
        hi = la2.astype(BF16)
        mid = (la2 - hi.astype(F32)).astype(BF16)
        sums = (jnp.dot(m_ref[...], hi, preferred_element_type=F32)
                + jnp.dot(m_ref[...], mid, preferred_element_type=F32))
        c_ref[rows, :] = sums[0:blk]
        return la2, sums

    def factors(ci, h, la2, sums):
        r0 = ci * blk
        rows = slice(r0, r0 + blk)
        ks = slice(h * C_DK, (h + 1) * C_DK)
        q = q_ref[rows, ks].astype(F32)
        k = k_ref[rows, ks].astype(F32)
        c2 = sums[0:blk, ks]
        small_e = {1: jnp.where(odd, la2[:, ks], 0.0)}
        for n, s in enumerate(GLA_MXU_LEVELS):
            small_e[s] = sums[(n + 1) * blk:(n + 2) * blk, ks]
        zs = [_level_factors(c_ref, c2, small_e, q, k, r0, ks, blk, s).astype(BF16)
              for s in levels]
        c_last = c_ref[r0 + blk - 1:r0 + blk, ks]
        qe = (q * jnp.exp2(c2)).astype(BF16)
        kd = (k * jnp.exp2(c_last - c2)).astype(BF16)
        diag = jnp.sum(q * k, axis=-1, keepdims=True)
        return zs, qe, kd, diag, jnp.exp2(c_last)

    def mix(ci, h, zs, qe, kd, diag, state_decay):
        rows = slice(ci * blk, (ci + 1) * blk)
        vs = slice(h * C_DV, (h + 1) * C_DV)
        a = jnp.where(eye, diag, 0.0)
        for li in reversed(range(len(levels))):
            gram = lax.dot_general(zs[li], zs[li], nt, preferred_element_type=F32)
            a = jnp.where(split >= levels[li], gram, a)
        vh = v_ref[rows, vs]
        st = st_ref[h]
        o = (jnp.dot(a.astype(BF16), vh, preferred_element_type=F32)
             + lax.dot_general(qe, st.astype(BF16), nt, preferred_element_type=F32))
        st_ref[h] = (st * state_decay
                     + lax.dot_general(vh, kd, tn, preferred_element_type=F32))
        on = o * lax.rsqrt(jnp.mean(o * o, axis=-1, keepdims=True) + NORM_EPS)
        y = (on * ng_ref[:, vs] * sg_ref[rows, vs].astype(F32)).astype(BF16)
        if h < C_HEADS // 2:
            ya_ref[rows, h * C_DV:(h + 1) * C_DV] = y
        else:
            hh = h - C_HEADS // 2
            yb_ref[rows, hh * C_DV:(hh + 1) * C_DV] = y

    items = [(ci, h) for ci in range(n_blocks) for h in range(C_HEADS)]
    sums = {0: range_sums(0)}
    ready = factors(0, 0, *sums[0])
    for n, (ci, h) in enumerate(items):
        if h == 0 and ci + 1 < n_blocks:
            sums[ci + 1] = range_sums(ci + 1)
        upcoming = None
        if n + 1 < len(items):
            nci, nh = items[n + 1]
            upcoming = factors(nci, nh, *sums[nci])
        mix(ci, h, *ready)
        ready = upcoming


def _gla(q, k, v, sg, la, norm_g, batch):
    t = q.shape[0]
    tc = TC_GLA
    nt = t // batch // tc
    tok = lambda n: pl.BlockSpec((tc, n), lambda b, i: (b * nt + i, 0))
    half = C_VAL_DIM // 2
    out = jax.ShapeDtypeStruct((t, half), BF16)
    return pl.pallas_call(
        _gla_kernel,
        grid=(batch, nt),
        in_specs=[tok(C_KEY_DIM), tok(C_KEY_DIM), tok(C_VAL_DIM), tok(C_VAL_DIM),
                  tok(C_KEY_DIM), _const_spec((1, C_VAL_DIM))],
        out_specs=[tok(half), tok(half)],
        out_shape=[out, out],
        scratch_shapes=[pltpu.VMEM((C_HEADS, C_DV, C_DK), F32),
                        pltpu.VMEM((tc, C_KEY_DIM), F32),
                        pltpu.VMEM(((1 + len(GLA_MXU_LEVELS)) * GLA_BLOCK, GLA_BLOCK), BF16)],
        compiler_params=_params(("parallel", "arbitrary")),
        name="l1_gla_scan",
    )(q, k, v, sg, la, norm_g)


def _rel_bias_row(rel_bias):
    h = rel_bias.shape[0]
    last = rel_bias[:, 2 * A_MAX_REL:]
    head = jnp.broadcast_to(last, (h, A_MAX_REL))
    mid = rel_bias[:, ::-1]
    tail = jnp.broadcast_to(last, (h, BIAS_ROW - 3 * A_MAX_REL - 1))
    return jnp.concatenate([head, mid, tail], axis=1)


def kernel(x, pre_mix_g, post_mix_g, pre_ffn_g, post_ffn_g, ab_w_in, a_rel_bias, b_ln_g,
           b_ln_b, b_w_s, b_b_s, ab_w_out, c_w_in, c_w_a2, c_b_a, c_norm_g, c_w_out,
           ffn_w_gate, ffn_w_up, ffn_w_down):
    batch, seq, d = x.shape
    x2d = x.reshape(batch * seq, d)
    vec = lambda a: a.reshape(1, -1)
    bf = lambda a: a.astype(BF16)
    depth = pre_mix_g.shape[0]
    gains = [g.reshape(depth, 1, d) for g in (post_mix_g, pre_ffn_g, post_ffn_g)]
    ffn_w = [bf(ffn_w_gate), bf(ffn_w_up), bf(ffn_w_down)]

    q, k, v, b_out = _l0_in(x2d, vec(pre_mix_g[0]), bf(ab_w_in), vec(b_ln_g[0]),
                            vec(b_ln_b[0]), b_w_s[0], b_b_s[0].T)
    a_out = _attention(q, k, v, _rel_bias_row(a_rel_bias[0]), batch)
    x2d = _mix_ffn(x2d, a_out, b_out, bf(ab_w_out), 0, *gains, *ffn_w)

    w_in1 = bf(jnp.pad(c_w_in[0], ((0, 0), (0, C_IN_PAD - C_IN))))
    q, k, v, sg, la = _l1_in(x2d, vec(pre_mix_g[1]), w_in1, bf(c_w_a2[0]), vec(c_b_a[0]))
    ya, yb = _gla(q, k, v, sg, la, vec(c_norm_g[0]), batch)
    x2d = _mix_ffn(x2d, ya, yb, bf(c_w_out), 1, *gains, *ffn_w)
    return x2d.reshape(batch, seq, d)
```

```python
import functools

import jax
import jax.numpy as jnp
from jax import lax
from jax.experimental import pallas as pl
from jax.experimental.pallas import tpu as pltpu

F32 = jnp.float32
BF16 = jnp.bfloat16

D_MODEL = 1024
CHUNK = 64
NORM_EPS = 1e-6
A_HEADS = 8
A_HEAD_DIM = 64
A_WIDTH = A_HEADS * A_HEAD_DIM
A_LEFT_CHUNKS = 8
A_MAX_REL = 256
B_GROUPS = 4
B_BLOCK = 128
B_WIDTH = D_MODEL // 2
AB_IN = 3 * A_WIDTH + 2 * B_WIDTH
C_HEADS = 4
C_KEY_DIM = D_MODEL // 2
C_VAL_DIM = D_MODEL
C_DK = C_KEY_DIM // C_HEADS
C_DV = C_VAL_DIM // C_HEADS
C_GATE_RANK = 16
C_GATE_TAU = 16.0
C_MAIN = 2 * C_KEY_DIM + 2 * C_VAL_DIM
C_IN = C_MAIN + C_GATE_RANK
D_FF = 2816

LANES = 128
C_IN_PAD = -(-C_IN // LANES) * LANES
VMEM_LIMIT_BYTES = 56 * 1024 * 1024

NEG_BIG = float(jnp.finfo(jnp.float32).min)
LOG2E = 1.4426950408889634

TM_IN0 = 1024
IN_SUBTILES = 4
TQ_ATTN = 256
ATTN_TILES = 2
TM_FFN = 1024
FFN_SUBTILES = 4
TM_IN1 = 1024
TC_GLA = 1024
GLA_BLOCK = 128
GLA_STAGE_LAGS = (1, 2)
ATTN_KEYS = TQ_ATTN + A_LEFT_CHUNKS * CHUNK
BIAS_ROW = 1024


def _rms_norm(x, g):
    ms = jnp.mean(x * x, axis=-1, keepdims=True)
    return x * lax.rsqrt(ms + NORM_EPS) * g


def _gelu(x):
    return 0.5 * x * (1.0 + lax.erf(x * (2.0 ** -0.5)))


def _const_spec(shape):
    nd = len(shape)
    return pl.BlockSpec(shape, lambda *_: (0,) * nd, pipeline_mode=pl.Buffered(1))


def _layer_spec(shape, layer, row_block=0):
    return pl.BlockSpec((None,) + tuple(shape), lambda *_: (layer, row_block, 0),
                        pipeline_mode=pl.Buffered(1))


def _params(sem):
    return pltpu.CompilerParams(dimension_semantics=sem,
                                vmem_limit_bytes=VMEM_LIMIT_BYTES)


def _l0_in_kernel(x_ref, g_ref, w_ref, lng_ref, lnb_ref, ws_ref, bst_ref,
                  q_ref, k_ref, v_ref, b_ref):
    sub = x_ref.shape[0] // IN_SUBTILES
    row = lax.broadcasted_iota(jnp.int32, (B_BLOCK, B_BLOCK), 0)
    col = lax.broadcasted_iota(jnp.int32, (B_BLOCK, B_BLOCK), 1)
    causal = col <= row
    w_gate = [jnp.where(causal, ws_ref[g], 0.0).astype(BF16) for g in range(B_GROUPS)]

    n_attn = 3 * A_WIDTH
    hidden = {}

    def project(r, part):
        rows = slice(r * sub, (r + 1) * sub)
        if r not in hidden:
            hidden[r] = _rms_norm(x_ref[rows, :], g_ref[...]).astype(BF16)
        cols = slice(n_attn, AB_IN) if part == "gmlp" else slice(0, n_attn)
        return jnp.dot(hidden[r], w_ref[:, cols], preferred_element_type=F32)

    def finish_attn(r, proj):
        rows = slice(r * sub, (r + 1) * sub)
        q_ref[rows, :] = (proj[:, 0:A_WIDTH] * (A_HEAD_DIM ** -0.5 * LOG2E)).astype(BF16)
        k_ref[rows, :] = proj[:, A_WIDTH:2 * A_WIDTH].astype(BF16)
        v_ref[rows, :] = proj[:, 2 * A_WIDTH:3 * A_WIDTH].astype(BF16)

    def finish_gmlp(r, proj):
        u = _gelu(proj[:, 0:B_WIDTH])
        zv = _gelu(proj[:, B_WIDTH:2 * B_WIDTH])
        mu = jnp.mean(zv, axis=-1, keepdims=True)
        d = zv - mu
        var = jnp.mean(d * d, axis=-1, keepdims=True)
        vln = (d * lax.rsqrt(var + NORM_EPS) * lng_ref[...] + lnb_ref[...]).astype(BF16)
        for g in range(B_GROUPS):
            bcol = bst_ref[:, g:g + 1]
            cs = slice(g * LANES, (g + 1) * LANES)
            for n in range(sub // B_BLOCK):
                rs = slice(n * B_BLOCK, (n + 1) * B_BLOCK)
                f = jnp.dot(w_gate[g], vln[rs, cs], preferred_element_type=F32) + bcol
                out_rows = slice(r * sub + n * B_BLOCK, r * sub + (n + 1) * B_BLOCK)
                b_ref[out_rows, cs] = (u[rs, cs] * f).astype(BF16)

    items = [(r, part) for r in range(IN_SUBTILES) for part in ("gmlp", "attn")]
    proj = project(*items[0])
    for n, (r, part) in enumerate(items):
        upcoming = project(*items[n + 1]) if n + 1 < len(items) else None
        (finish_gmlp if part == "gmlp" else finish_attn)(r, proj)
        proj = upcoming


def _l0_in(x2d, g, w_in, ln_g, ln_b, w_s, b_s_t):
    t = x2d.shape[0]
    tm = TM_IN0
    tok = lambda n: pl.BlockSpec((tm, n), lambda i: (i, 0))
    out = jax.ShapeDtypeStruct((t, A_WIDTH), BF16)
    return pl.pallas_call(
        _l0_in_kernel,
        grid=(t // tm,),
        in_specs=[tok(D_MODEL), _const_spec((1, D_MODEL)),
                  _layer_spec((D_MODEL, AB_IN), 0),
                  _const_spec((1, B_WIDTH)), _const_spec((1, B_WIDTH)),
                  _const_spec((B_GROUPS, B_BLOCK, B_BLOCK)), _const_spec((B_BLOCK, B_GROUPS))],
        out_specs=[tok(A_WIDTH)] * 4,
        out_shape=[out] * 4,
        compiler_params=_params(("parallel",)),
        name="l0_in_proj_gmlp",
    )(x2d, g, w_in, ln_g, ln_b, w_s, b_s_t)


def _attn_kernel(q_ref, kp_ref, kc_ref, vp_ref, vc_ref, rb_ref, o_ref, bias_ref):
    i = pl.program_id(1)
    tq = TQ_ATTN
    nk = ATTN_KEYS
    step = ATTN_TILES * tq
    left = A_LEFT_CHUNKS * CHUNK

    @pl.when((i - 1) * step < left)
    def _build_bias():
        qq = lax.broadcasted_iota(jnp.int32, (tq, nk), 0)
        kk = lax.broadcasted_iota(jnp.int32, (tq, nk), 1)
        band_lo = (qq // CHUNK) * CHUNK
        in_band = (kk >= band_lo) & (kk < band_lo + (A_LEFT_CHUNKS + 1) * CHUNK)
        for h in range(A_HEADS):
            r = jnp.broadcast_to(rb_ref[h:h + 1, :], (tq, BIAS_ROW))
            t = pltpu.roll(r, 0, 1, stride=1, stride_axis=0)[:, :nk] * LOG2E
            for tile in range(ATTN_TILES):
                first_key = i * step + tile * tq - left
                bias_ref[tile, h] = jnp.where(in_band & (kk + first_key >= 0), t, NEG_BIG)

    kcat = jnp.concatenate([kp_ref[...], kc_ref[...]], axis=0)
    vcat = jnp.concatenate([vp_ref[...], vc_ref[...]], axis=0)
    lane = lax.broadcasted_iota(jnp.int32, (1, LANES), 1)
    lo = lane < A_HEAD_DIM
    nt = (((1,), (1,)), ((), ()))

    def head_sel(h):
        return lo if h % 2 == 0 else jnp.logical_not(lo)

    def scores(tile, h):
        cs = slice((h // 2) * LANES, (h // 2 + 1) * LANES)
        qp = q_ref[tile * tq:(tile + 1) * tq, cs]
        qh = jnp.where(head_sel(h), qp, jnp.zeros_like(qp))
        return lax.dot_general(qh, kcat[tile * tq:tile * tq + nk, cs], nt,
                               preferred_element_type=F32)

    def biased_scores(tile, h, s):
        s = s + bias_ref[tile, h]
        return s, jnp.max(s, axis=-1, keepdims=True)

    def softmax_numerator(s, m):
        return jnp.exp2(s - m).astype(BF16)

    def weighted_values(tile, h, e):
        cs = slice((h // 2) * LANES, (h // 2 + 1) * LANES)
        ones_lane = A_HEAD_DIM if h % 2 == 0 else 0
        vh = jnp.where(head_sel(h), vcat[tile * tq:tile * tq + nk, cs],
                       (lane == ones_lane).astype(BF16))
        o = jnp.dot(e, vh, preferred_element_type=F32)
        return o / o[:, ones_lane:ones_lane + 1]

    items = [(tile, h) for tile in range(ATTN_TILES) for h in range(A_HEADS)]
    s_val = {}
    b_val = {}
    e_val = {}
    outs = {}
    for t in range(len(items) + 3):
        if t < len(items):
            s_val[t] = scores(*items[t])
        if 0 <= t - 1 < len(items):
            b_val[t - 1] = biased_scores(*items[t - 1], s_val.pop(t - 1))
        if 0 <= t - 2 < len(items):
            e_val[t - 2] = softmax_numerator(*b_val.pop(t - 2))
        n = t - 3
        if 0 <= n < len(items):
            tile, h = items[n]
            outs[h] = weighted_values(tile, h, e_val.pop(n))
            if h % 2 == 1:
                cs = slice((h // 2) * LANES, (h // 2 + 1) * LANES)
                o_ref[tile * tq:(tile + 1) * tq, cs] = jnp.where(
                    lo, outs.pop(h - 1), outs.pop(h)).astype(BF16)


def _attention(q, k, v, rb_row, batch):
    t = q.shape[0]
    step = ATTN_TILES * TQ_ATTN
    assert step == A_LEFT_CHUNKS * CHUNK
    nt = t // batch // step
    cur = pl.BlockSpec((step, A_WIDTH), lambda b, i: (b * nt + i, 0))
    prev = pl.BlockSpec((step, A_WIDTH), lambda b, i: (b * nt + jnp.maximum(i - 1, 0), 0))
    return pl.pallas_call(
        _attn_kernel,
        grid=(batch, nt),
        in_specs=[cur, prev, cur, prev, cur, _const_spec((A_HEADS, BIAS_ROW))],
        out_specs=cur,
        out_shape=jax.ShapeDtypeStruct((t, A_WIDTH), BF16),
        scratch_shapes=[pltpu.VMEM((ATTN_TILES, A_HEADS, TQ_ATTN, ATTN_KEYS), F32)],
        compiler_params=_params(("parallel", "arbitrary")),
        name="l0_band_attention",
    )(q, k, k, v, v, rb_row)


def _mix_ffn_kernel(x_ref, a_ref, b_ref, woa_ref, wob_ref, gpm_ref, gpf_ref, gpo_ref,
                    wg_ref, wu_ref, wd_ref, o_ref):
    sub = x_ref.shape[0] // FFN_SUBTILES
    tiles = [slice(r * sub, (r + 1) * sub) for r in range(FFN_SUBTILES)]

    def mix_proj(rows):
        return (jnp.dot(a_ref[rows, :], woa_ref[...], preferred_element_type=F32)
                + jnp.dot(b_ref[rows, :], wob_ref[...], preferred_element_type=F32))

    def norms(rows, m):
        x1 = x_ref[rows, :] + _rms_norm(m, gpm_ref[...])
        return x1, _rms_norm(x1, gpf_ref[...]).astype(BF16)

    def gate_up(h):
        return (jnp.dot(h, wg_ref[...], preferred_element_type=F32),
                jnp.dot(h, wu_ref[...], preferred_element_type=F32))

    def activation(gu):
        gate, up = gu
        return (gate * jax.nn.sigmoid(gate) * up).astype(BF16)

    def down(act):
        return jnp.dot(act, wd_ref[...], preferred_element_type=F32)

    def finish(rows, x1, y):
        o_ref[rows, :] = x1 + _rms_norm(y, gpo_ref[...])

    stages = [mix_proj, norms, gate_up, activation, down, finish]
    state = [None] * FFN_SUBTILES
    x1s = [None] * FFN_SUBTILES
    for step in range(len(stages) + FFN_SUBTILES - 1):
        for r in range(FFN_SUBTILES):
            t = step - r
            if t < 0 or t >= len(stages):
                continue
            rows = tiles[r]
            if t == 0:
                state[r] = mix_proj(rows)
            elif t == 1:
                x1s[r], state[r] = norms(rows, state[r])
            elif t == 2:
                state[r] = gate_up(state[r])
            elif t == 3:
                state[r] = activation(state[r])
            elif t == 4:
                state[r] = down(state[r])
            else:
                finish(rows, x1s[r], state[r])


def _mix_ffn(x2d, a, b, wo, layer, g_post_mix, g_pre_ffn, g_post_ffn, w_gate, w_up, w_down):
    t = x2d.shape[0]
    tm = TM_FFN
    half = wo.shape[1] // 2
    tok = lambda n: pl.BlockSpec((tm, n), lambda i: (i, 0))
    vec = _layer_spec((1, D_MODEL), layer)
    return pl.pallas_call(
        _mix_ffn_kernel,
        grid=(t // tm,),
        in_specs=[tok(D_MODEL), tok(half), tok(half),
                  _layer_spec((half, D_MODEL), 0, 0), _layer_spec((half, D_MODEL), 0, 1),
                  vec, vec, vec,
                  _layer_spec((D_MODEL, D_FF), layer), _layer_spec((D_MODEL, D_FF), layer),
                  _layer_spec((D_FF, D_MODEL), layer)],
        out_specs=tok(D_MODEL),
        out_shape=jax.ShapeDtypeStruct((t, D_MODEL), F32),
        compiler_params=_params(("parallel",)),
        name="mix_out_ffn",
    )(x2d, a, b, wo, wo, g_post_mix, g_pre_ffn, g_post_ffn, w_gate, w_up, w_down)


def _l1_in_kernel(x_ref, g_ref, w_ref, wa2_ref, ba_ref,
                  q_ref, k_ref, v_ref, sg_ref, la_ref):
    sub = x_ref.shape[0] // IN_SUBTILES

    n_qkv = 2 * C_KEY_DIM + C_VAL_DIM
    hidden = {}

    def project(r, part):
        rows = slice(r * sub, (r + 1) * sub)
        if r not in hidden:
            hidden[r] = _rms_norm(x_ref[rows, :], g_ref[...]).astype(BF16)
        cols = slice(n_qkv, C_IN_PAD) if part == "gates" else slice(0, n_qkv)
        return jnp.dot(hidden[r], w_ref[:, cols], preferred_element_type=F32)

    def finish_qkv(r, proj):
        rows = slice(r * sub, (r + 1) * sub)
        q_ref[rows, :] = (proj[:, 0:C_KEY_DIM] * (C_DK ** -0.5)).astype(BF16)
        k_ref[rows, :] = proj[:, C_KEY_DIM:2 * C_KEY_DIM].astype(BF16)
        v_ref[rows, :] = proj[:, 2 * C_KEY_DIM:n_qkv].astype(BF16)

    def finish_gates(r, proj):
        rows = slice(r * sub, (r + 1) * sub)
        a_low = proj[:, C_VAL_DIM:C_VAL_DIM + C_GATE_RANK]
        z = (jnp.dot(a_low.astype(BF16), wa2_ref[...], preferred_element_type=F32)
             + ba_ref[...])
        log_sig = jnp.minimum(z, 0.0) - jnp.log1p(jnp.exp(-jnp.abs(z)))
        la_ref[rows, :] = log_sig * (1.0 / C_GATE_TAU)
        gg = proj[:, 0:C_VAL_DIM]
        sg_ref[rows, :] = (gg * jax.nn.sigmoid(gg)).astype(BF16)

    items = [(r, part) for r in range(IN_SUBTILES) for part in ("gates", "qkv")]
    proj = project(*items[0])
    for n, (r, part) in enumerate(items):
        upcoming = project(*items[n + 1]) if n + 1 < len(items) else None
        (finish_gates if part == "gates" else finish_qkv)(r, proj)
        proj = upcoming


def _l1_in(x2d, g, w_in, w_a2, b_a):
    t = x2d.shape[0]
    tm = TM_IN1
    tok = lambda n: pl.BlockSpec((tm, n), lambda i: (i, 0))
    return pl.pallas_call(
        _l1_in_kernel,
        grid=(t // tm,),
        in_specs=[tok(D_MODEL), _const_spec((1, D_MODEL)), _const_spec((D_MODEL, C_IN_PAD)),
                  _const_spec((C_GATE_RANK, C_KEY_DIM)), _const_spec((1, C_KEY_DIM))],
        out_specs=[tok(C_KEY_DIM), tok(C_KEY_DIM), tok(C_VAL_DIM), tok(C_VAL_DIM),
                   tok(C_KEY_DIM)],
        out_shape=[jax.ShapeDtypeStruct((t, C_KEY_DIM), BF16),
                   jax.ShapeDtypeStruct((t, C_KEY_DIM), BF16),
                   jax.ShapeDtypeStruct((t, C_VAL_DIM), BF16),
                   jax.ShapeDtypeStruct((t, C_VAL_DIM), BF16),
                   jax.ShapeDtypeStruct((t, C_KEY_DIM), F32)],
        compiler_params=_params(("parallel",)),
        name="l1_in_proj_gate",
    )(x2d, g, w_in, w_a2, b_a)


def _gla_levels(block):
    levels = []
    s = block // 2
    while s >= 1:
        levels.append(s)
        s //= 2
    return tuple(levels)


GLA_MXU_LEVELS = (4, 2)


def _range_sum_matrix(block):
    row = lax.broadcasted_iota(jnp.int32, (block, block), 0)
    col = lax.broadcasted_iota(jnp.int32, (block, block), 1)
    mats = [(col <= row).astype(F32)]
    for s in GLA_MXU_LEVELS:
        bnd = (row & (-2 * s)) + (s - 1)
        upper = (row & s) != 0
        first = jnp.where(upper, bnd, row)
        last = jnp.where(upper, row, bnd)
        mats.append(((col > first) & (col <= last)).astype(F32))
    return jnp.concatenate(mats, axis=0).astype(BF16)


def _level_factors(c_ref, c2, small_e, q, k, r0, ks, block, s):
    if s >= 8:
        parts = []
        for p in range(block // (2 * s)):
            lo = slice(2 * s * p, 2 * s * p + s)
            up = slice(2 * s * p + s, 2 * s * (p + 1))
            b = r0 + 2 * s * p + s - 1
            ref_row = c_ref[b:b + 1, ks]
            parts.append(k[lo] * jnp.exp2(ref_row - c2[lo]))
            parts.append(q[up] * jnp.exp2(c2[up] - ref_row))
        return jnp.concatenate(parts, axis=0)
    rowv = lax.broadcasted_iota(jnp.int32, (block, 1), 0)
    upper = (rowv & s) != 0
    return jnp.where(upper, q, k) * jnp.exp2(small_e[s])


def _gla_kernel(q_ref, k_ref, v_ref, sg_ref, la_ref, ng_ref, ya_ref, yb_ref,
                st_ref, c_ref, m_ref):
    blk = GLA_BLOCK

    @pl.when(pl.program_id(1) == 0)
    def _reset():
        st_ref[...] = jnp.zeros_like(st_ref)
        m_ref[...] = _range_sum_matrix(blk)

    levels = _gla_levels(blk)
    row = lax.broadcasted_iota(jnp.int32, (blk, blk), 0)
    col = lax.broadcasted_iota(jnp.int32, (blk, blk), 1)
    eye = col == row
    split = jnp.where(row > col, row ^ col, 0)
    odd = (lax.broadcasted_iota(jnp.int32, (blk, 1), 0) & 1) != 0
    nt = (((1,), (1,)), ((), ()))
    tn = (((0,), (0,)), ((), ()))
    n_blocks = q_ref.shape[0] // blk

    def range_sums(ci):
        rows = slice(ci * blk, (ci + 1) * blk)
        la2 = la_ref[rows, :] * LOG2E
        hi = la2.astype(BF16)
        mid = (la2 - hi.astype(F32)).astype(BF16)
        sums = (jnp.dot(m_ref[...], hi, preferred_element_type=F32)
                + jnp.dot(m_ref[...], mid, preferred_element_type=F32))
        c_ref[rows, :] = sums[0:blk]
        return la2, sums

    def factors(ci, h, la2, sums):
        r0 = ci * blk
        rows = slice(r0, r0 + blk)
        ks = slice(h * C_DK, (h + 1) * C_DK)
        q = q_ref[rows, ks].astype(F32)
        k = k_ref[rows, ks].astype(F32)
        c2 = sums[0:blk, ks]
        small_e = {1: jnp.where(odd, la2[:, ks], 0.0)}
        for n, s in enumerate(GLA_MXU_LEVELS):
            small_e[s] = sums[(n + 1) * blk:(n + 2) * blk, ks]
        zs = [_level_factors(c_ref, c2, small_e, q, k, r0, ks, blk, s).astype(BF16)
              for s in levels]
        c_last = c_ref[r0 + blk - 1:r0 + blk, ks]
        qe = (q * jnp.exp2(c2)).astype(BF16)
        kd = (k * jnp.exp2(c_last - c2)).astype(BF16)
        diag = jnp.sum(q * k, axis=-1, keepdims=True)
        return zs, qe, kd, diag, jnp.exp2(c_last)

    def intra_scores(zs, qe, kd, diag, state_decay):
        a = jnp.where(eye, diag, 0.0)
        for li in reversed(range(len(levels))):
            gram = lax.dot_general(zs[li], zs[li], nt, preferred_element_type=F32)
            a = jnp.where(split >= levels[li], gram, a)
        return a.astype(BF16), qe, kd, state_decay

    def mix(ci, h, a, qe, kd, state_decay):
        rows = slice(ci * blk, (ci + 1) * blk)
        vs = slice(h * C_DV, (h + 1) * C_DV)
        vh = v_ref[rows, vs]
        st = st_ref[h]
        o = (jnp.dot(a, vh, preferred_element_type=F32)
             + lax.dot_general(qe, st.astype(BF16), nt, preferred_element_type=F32))
        st_ref[h] = (st * state_decay
                     + lax.dot_general(vh, kd, tn, preferred_element_type=F32))
        on = o * lax.rsqrt(jnp.mean(o * o, axis=-1, keepdims=True) + NORM_EPS)
        y = (on * ng_ref[:, vs] * sg_ref[rows, vs].astype(F32)).astype(BF16)
        if h < C_HEADS // 2:
            ya_ref[rows, h * C_DV:(h + 1) * C_DV] = y
        else:
            hh = h - C_HEADS // 2
            yb_ref[rows, hh * C_DV:(hh + 1) * C_DV] = y

    items = [(ci, h) for ci in range(n_blocks) for h in range(C_HEADS)]
    sums = {0: range_sums(0)}
    fact = {}
    scored = {}
    lag_scores, lag_mix = GLA_STAGE_LAGS
    for t in range(len(items) + lag_mix):
        if t < len(items):
            ci, h = items[t]
            if h == 0 and ci + 1 < n_blocks:
                sums[ci + 1] = range_sums(ci + 1)
            fact[t] = factors(ci, h, *sums[ci])
        if 0 <= t - lag_scores < len(items):
            scored[t - lag_scores] = intra_scores(*fact.pop(t - lag_scores))
        if 0 <= t - lag_mix < len(items):
            mix(*items[t - lag_mix], *scored.pop(t - lag_mix))


def _gla(q, k, v, sg, la, norm_g, batch):
    t = q.shape[0]
    tc = TC_GLA
    nt = t // batch // tc
    tok = lambda n: pl.BlockSpec((tc, n), lambda b, i: (b * nt + i, 0))
    half = C_VAL_DIM // 2
    out = jax.ShapeDtypeStruct((t, half), BF16)
    return pl.pallas_call(
        _gla_kernel,
        grid=(batch, nt),
        in_specs=[tok(C_KEY_DIM), tok(C_KEY_DIM), tok(C_VAL_DIM), tok(C_VAL_DIM),
                  tok(C_KEY_DIM), _const_spec((1, C_VAL_DIM))],
        out_specs=[tok(half), tok(half)],
        out_shape=[out, out],
        scratch_shapes=[pltpu.VMEM((C_HEADS, C_DV, C_DK), F32),
                        pltpu.VMEM((tc, C_KEY_DIM), F32),
                        pltpu.VMEM(((1 + len(GLA_MXU_LEVELS)) * GLA_BLOCK, GLA_BLOCK), BF16)],
        compiler_params=_params(("parallel", "arbitrary")),
        name="l1_gla_scan",
    )(q, k, v, sg, la, norm_g)


def _rel_bias_row(rel_bias):
    h = rel_bias.shape[0]
    last = rel_bias[:, 2 * A_MAX_REL:]
    head = jnp.broadcast_to(last, (h, A_MAX_REL))
    mid = rel_bias[:, ::-1]
    tail = jnp.broadcast_to(last, (h, BIAS_ROW - 3 * A_MAX_REL - 1))
    return jnp.concatenate([head, mid, tail], axis=1)


def kernel(x, pre_mix_g, post_mix_g, pre_ffn_g, post_ffn_g, ab_w_in, a_rel_bias, b_ln_g,
           b_ln_b, b_w_s, b_b_s, ab_w_out, c_w_in, c_w_a2, c_b_a, c_norm_g, c_w_out,
           ffn_w_gate, ffn_w_up, ffn_w_down):
    batch, seq, d = x.shape
    x2d = x.reshape(batch * seq, d)
    vec = lambda a: a.reshape(1, -1)
    bf = lambda a: a.astype(BF16)
    depth = pre_mix_g.shape[0]
    gains = [g.reshape(depth, 1, d) for g in (post_mix_g, pre_ffn_g, post_ffn_g)]
    ffn_w = [bf(ffn_w_gate), bf(ffn_w_up), bf(ffn_w_down)]

    q, k, v, b_out = _l0_in(x2d, vec(pre_mix_g[0]), bf(ab_w_in), vec(b_ln_g[0]),
                            vec(b_ln_b[0]), b_w_s[0], b_b_s[0].T)
    a_out = _attention(q, k, v, _rel_bias_row(a_rel_bias[0]), batch)
    x2d = _mix_ffn(x2d, a_out, b_out, bf(ab_w_out), 0, *gains, *ffn_w)

    w_in1 = bf(jnp.pad(c_w_in[0], ((0, 0), (0, C_IN_PAD - C_IN))))
    q, k, v, sg, la = _l1_in(x2d, vec(pre_mix_g[1]), w_in1, bf(c_w_a2[0]), vec(c_b_a[0]))
    ya, yb = _gla(q, k, v, sg, la, vec(c_norm_g[0]), batch)
    x2d = _mix_ffn(x2d, ya, yb, bf(c_w_out), 1, *gains, *ffn_w)
    return x2d.reshape(batch, seq, d)
```

```python
import functools

import jax
import jax.numpy as jnp
from jax import lax
from jax.experimental import pallas as pl
from jax.experimental.pallas import tpu as pltpu

F32 = jnp.float32
BF16 = jnp.bfloat16

D_MODEL = 1024
CHUNK = 64
NORM_EPS = 1e-6
A_HEADS = 8
A_HEAD_DIM = 64
A_WIDTH = A_HEADS * A_HEAD_DIM
A_LEFT_CHUNKS = 8
A_MAX_REL = 256
B_GROUPS = 4
B_BLOCK = 128
B_WIDTH = D_MODEL // 2
AB_IN = 3 * A_WIDTH + 2 * B_WIDTH
C_HEADS = 4
C_KEY_DIM = D_MODEL // 2
C_VAL_DIM = D_MODEL
C_DK = C_KEY_DIM // C_HEADS
C_DV = C_VAL_DIM // C_HEADS
C_GATE_RANK = 16
C_GATE_TAU = 16.0
C_MAIN = 2 * C_KEY_DIM + 2 * C_VAL_DIM
C_IN = C_MAIN + C_GATE_RANK
D_FF = 2816

LANES = 128
C_IN_PAD = -(-C_IN // LANES) * LANES
VMEM_LIMIT_BYTES = 56 * 1024 * 1024

NEG_BIG = float(jnp.finfo(jnp.float32).min)
LOG2E = 1.4426950408889634

TM_IN0 = 1024
IN_SUBTILES = 4
TQ_ATTN = 256
ATTN_TILES = 2
TM_FFN = 1024
FFN_SUBTILES = 4
FFN_CAST_CHUNKS = 16
TM_IN1 = 1024
TC_GLA = 1024
GLA_BLOCK = 128
GLA_STAGE_LAGS = (1, 2)
ATTN_KEYS = TQ_ATTN + A_LEFT_CHUNKS * CHUNK
BIAS_ROW = 1024


def _rms_norm(x, g):
    ms = jnp.mean(x * x, axis=-1, keepdims=True)
    return x * lax.rsqrt(ms + NORM_EPS) * g


def _gelu(x):
    return 0.5 * x * (1.0 + lax.erf(x * (2.0 ** -0.5)))


def _const_spec(shape):
    nd = len(shape)
    return pl.BlockSpec(shape, lambda *_: (0,) * nd, pipeline_mode=pl.Buffered(1))


def _layer_spec(shape, layer, row_block=0):
    return pl.BlockSpec((None,) + tuple(shape), lambda *_: (layer, row_block, 0),
                        pipeline_mode=pl.Buffered(1))


def _params(sem):
    return pltpu.CompilerParams(dimension_semantics=sem,
                                vmem_limit_bytes=VMEM_LIMIT_BYTES)


def _l0_in_kernel(x_ref, g_ref, w_ref, lng_ref, lnb_ref, ws_ref, bst_ref,
                  q_ref, k_ref, v_ref, b_ref):
    sub = x_ref.shape[0] // IN_SUBTILES
    row = lax.broadcasted_iota(jnp.int32, (B_BLOCK, B_BLOCK), 0)
    col = lax.broadcasted_iota(jnp.int32, (B_BLOCK, B_BLOCK), 1)
    causal = col <= row
    w_gate = [jnp.where(causal, ws_ref[g], 0.0).astype(BF16) for g in range(B_GROUPS)]

    n_attn = 3 * A_WIDTH
    hidden = {}

    def project(r, part):
        rows = slice(r * sub, (r + 1) * sub)
        if r not in hidden:
            hidden[r] = _rms_norm(x_ref[rows, :], g_ref[...]).astype(BF16)
        cols = slice(n_attn, AB_IN) if part == "gmlp" else slice(0, n_attn)
        return jnp.dot(hidden[r], w_ref[:, cols], preferred_element_type=F32)

    def finish_attn(r, proj):
        rows = slice(r * sub, (r + 1) * sub)
        q_ref[rows, :] = (proj[:, 0:A_WIDTH] * (A_HEAD_DIM ** -0.5 * LOG2E)).astype(BF16)
        k_ref[rows, :] = proj[:, A_WIDTH:2 * A_WIDTH].astype(BF16)
        v_ref[rows, :] = proj[:, 2 * A_WIDTH:3 * A_WIDTH].astype(BF16)

    def finish_gmlp(r, proj):
        u = _gelu(proj[:, 0:B_WIDTH])
        zv = _gelu(proj[:, B_WIDTH:2 * B_WIDTH])
        mu = jnp.mean(zv, axis=-1, keepdims=True)
        d = zv - mu
        var = jnp.mean(d * d, axis=-1, keepdims=True)
        vln = (d * lax.rsqrt(var + NORM_EPS) * lng_ref[...] + lnb_ref[...]).astype(BF16)
        for g in range(B_GROUPS):
            bcol = bst_ref[:, g:g + 1]
            cs = slice(g * LANES, (g + 1) * LANES)
            for n in range(sub // B_BLOCK):
                rs = slice(n * B_BLOCK, (n + 1) * B_BLOCK)
                f = jnp.dot(w_gate[g], vln[rs, cs], preferred_element_type=F32) + bcol
                out_rows = slice(r * sub + n * B_BLOCK, r * sub + (n + 1) * B_BLOCK)
                b_ref[out_rows, cs] = (u[rs, cs] * f).astype(BF16)

    items = [(r, part) for r in range(IN_SUBTILES) for part in ("gmlp", "attn")]
    proj = project(*items[0])
    for n, (r, part) in enumerate(items):
        upcoming = project(*items[n + 1]) if n + 1 < len(items) else None
        (finish_gmlp if part == "gmlp" else finish_attn)(r, proj)
        proj = upcoming


def _l0_in(x2d, g, w_in, ln_g, ln_b, w_s, b_s_t):
    t = x2d.shape[0]
    tm = TM_IN0
    tok = lambda n: pl.BlockSpec((tm, n), lambda i: (i, 0))
    out = jax.ShapeDtypeStruct((t, A_WIDTH), BF16)
    return pl.pallas_call(
        _l0_in_kernel,
        grid=(t // tm,),
        in_specs=[tok(D_MODEL), _const_spec((1, D_MODEL)),
                  _layer_spec((D_MODEL, AB_IN), 0),
                  _const_spec((1, B_WIDTH)), _const_spec((1, B_WIDTH)),
                  _const_spec((B_GROUPS, B_BLOCK, B_BLOCK)), _const_spec((B_BLOCK, B_GROUPS))],
        out_specs=[tok(A_WIDTH)] * 4,
        out_shape=[out] * 4,
        compiler_params=_params(("parallel",)),
        name="l0_in_proj_gmlp",
    )(x2d, g, w_in, ln_g, ln_b, w_s, b_s_t)


def _attn_kernel(q_ref, kp_ref, kc_ref, vp_ref, vc_ref, rb_ref, o_ref, bias_ref):
    i = pl.program_id(1)
    tq = TQ_ATTN
    nk = ATTN_KEYS
    step = ATTN_TILES * tq
    left = A_LEFT_CHUNKS * CHUNK

    @pl.when((i - 1) * step < left)
    def _build_bias():
        qq = lax.broadcasted_iota(jnp.int32, (tq, nk), 0)
        kk = lax.broadcasted_iota(jnp.int32, (tq, nk), 1)
        band_lo = (qq // CHUNK) * CHUNK
        in_band = (kk >= band_lo) & (kk < band_lo + (A_LEFT_CHUNKS + 1) * CHUNK)
        for h in range(A_HEADS):
            r = jnp.broadcast_to(rb_ref[h:h + 1, :], (tq, BIAS_ROW))
            t = pltpu.roll(r, 0, 1, stride=1, stride_axis=0)[:, :nk] * LOG2E
            for tile in range(ATTN_TILES):
                first_key = i * step + tile * tq - left
                bias_ref[tile, h] = jnp.where(in_band & (kk + first_key >= 0), t, NEG_BIG)

    kcat = jnp.concatenate([kp_ref[...], kc_ref[...]], axis=0)
    vcat = jnp.concatenate([vp_ref[...], vc_ref[...]], axis=0)
    lane = lax.broadcasted_iota(jnp.int32, (1, LANES), 1)
    lo = lane < A_HEAD_DIM
    nt = (((1,), (1,)), ((), ()))

    def head_sel(h):
        return lo if h % 2 == 0 else jnp.logical_not(lo)

    def scores(tile, h):
        cs = slice((h // 2) * LANES, (h // 2 + 1) * LANES)
        qp = q_ref[tile * tq:(tile + 1) * tq, cs]
        qh = jnp.where(head_sel(h), qp, jnp.zeros_like(qp))
        return lax.dot_general(qh, kcat[tile * tq:tile * tq + nk, cs], nt,
                               preferred_element_type=F32)

    def biased_scores(tile, h, s):
        s = s + bias_ref[tile, h]
        return s, jnp.max(s, axis=-1, keepdims=True)

    def softmax_numerator(s, m):
        return jnp.exp2(s - m).astype(BF16)

    def weighted_values(tile, h, e):
        cs = slice((h // 2) * LANES, (h // 2 + 1) * LANES)
        ones_lane = A_HEAD_DIM if h % 2 == 0 else 0
        vh = jnp.where(head_sel(h), vcat[tile * tq:tile * tq + nk, cs],
                       (lane == ones_lane).astype(BF16))
        o = jnp.dot(e, vh, preferred_element_type=F32)
        return o / o[:, ones_lane:ones_lane + 1]

    items = [(tile, h) for tile in range(ATTN_TILES) for h in range(A_HEADS)]
    s_val = {}
    b_val = {}
    e_val = {}
    outs = {}
    for t in range(len(items) + 3):
        if t < len(items):
            s_val[t] = scores(*items[t])
        if 0 <= t - 1 < len(items):
            b_val[t - 1] = biased_scores(*items[t - 1], s_val.pop(t - 1))
        if 0 <= t - 2 < len(items):
            e_val[t - 2] = softmax_numerator(*b_val.pop(t - 2))
        n = t - 3
        if 0 <= n < len(items):
            tile, h = items[n]
            outs[h] = weighted_values(tile, h, e_val.pop(n))
            if h % 2 == 1:
                cs = slice((h // 2) * LANES, (h // 2 + 1) * LANES)
                o_ref[tile * tq:(tile + 1) * tq, cs] = jnp.where(
                    lo, outs.pop(h - 1), outs.pop(h)).astype(BF16)


def _attention(q, k, v, rb_row, batch):
    t = q.shape[0]
    step = ATTN_TILES * TQ_ATTN
    assert step == A_LEFT_CHUNKS * CHUNK
    nt = t // batch // step
    cur = pl.BlockSpec((step, A_WIDTH), lambda b, i: (b * nt + i, 0))
    prev = pl.BlockSpec((step, A_WIDTH), lambda b, i: (b * nt + jnp.maximum(i - 1, 0), 0))
    return pl.pallas_call(
        _attn_kernel,
        grid=(batch, nt),
        in_specs=[cur, prev, cur, prev, cur, _const_spec((A_HEADS, BIAS_ROW))],
        out_specs=cur,
        out_shape=jax.ShapeDtypeStruct((t, A_WIDTH), BF16),
        scratch_shapes=[pltpu.VMEM((ATTN_TILES, A_HEADS, TQ_ATTN, ATTN_KEYS), F32)],
        compiler_params=_params(("parallel", "arbitrary")),
        name="l0_band_attention",
    )(q, k, k, v, v, rb_row)


def _cast_rows_to_bf16(src, dst, stage, sem):
    chunk = stage.shape[1]
    n = src.shape[0] // chunk

    def copy(i, slot):
        return pltpu.make_async_copy(src.at[pl.ds(i * chunk, chunk), :], stage.at[slot],
                                     sem.at[slot])

    copy(0, 0).start()

    def body(i, carry):
        slot = lax.rem(i, 2)

        @pl.when(i + 1 < n)
        def _prefetch():
            copy(i + 1, 1 - slot).start()

        copy(i, slot).wait()
        dst[pl.ds(pl.multiple_of(i * chunk, chunk), chunk), :] = stage[slot].astype(BF16)
        return carry

    lax.fori_loop(0, n, body, 0)


def _mix_ffn_kernel(x_ref, a_ref, b_ref, woa_ref, wob_ref, gpm_ref, gpf_ref, gpo_ref,
                    wg_hbm, wu_hbm, wd_hbm, o_ref,
                    wg_ref, wu_ref, wd_ref, stage_in, stage_out, sem, *, layer):
    @pl.when(pl.program_id(0) == 0)
    def _load_weights():
        _cast_rows_to_bf16(wg_hbm.at[layer], wg_ref, stage_in, sem)
        _cast_rows_to_bf16(wu_hbm.at[layer], wu_ref, stage_in, sem)
        _cast_rows_to_bf16(wd_hbm.at[layer], wd_ref, stage_out, sem)

    sub = x_ref.shape[0] // FFN_SUBTILES
    tiles = [slice(r * sub, (r + 1) * sub) for r in range(FFN_SUBTILES)]

    def mix_proj(rows, _):
        return (jnp.dot(a_ref[rows, :], woa_ref[...], preferred_element_type=F32)
                + jnp.dot(b_ref[rows, :], wob_ref[...], preferred_element_type=F32))

    def norms(rows, m):
        x1 = x_ref[rows, :] + _rms_norm(m, gpm_ref[...])
        return x1, _rms_norm(x1, gpf_ref[...]).astype(BF16)

    def gate_up(rows, st):
        x1, h = st
        return (x1, jnp.dot(h, wg_ref[...], preferred_element_type=F32),
                jnp.dot(h, wu_ref[...], preferred_element_type=F32))

    def activation(rows, st):
        x1, gate, up = st
        return x1, (gate * jax.nn.sigmoid(gate) * up).astype(BF16)

    def down(rows, st):
        x1, act = st
        return x1, jnp.dot(act, wd_ref[...], preferred_element_type=F32)

    def finish(rows, st):
        x1, y = st
        o_ref[rows, :] = x1 + _rms_norm(y, gpo_ref[...])
        return None

    stages = [mix_proj, norms, gate_up, activation, down, finish]
    state = [None] * FFN_SUBTILES
    for step in range(len(stages) + FFN_SUBTILES - 1):
        for r in range(FFN_SUBTILES):
            t = step - r
            if 0 <= t < len(stages):
                state[r] = stages[t](tiles[r], state[r])


def _mix_ffn(x2d, a, b, wo, layer, g_post_mix, g_pre_ffn, g_post_ffn, w_gate, w_up, w_down):
    t = x2d.shape[0]
    tm = TM_FFN
    half = wo.shape[1] // 2
    tok = lambda n: pl.BlockSpec((tm, n), lambda i: (i, 0))
    vec = _layer_spec((1, D_MODEL), layer)
    hbm = pl.BlockSpec(memory_space=pl.ANY)
    return pl.pallas_call(
        functools.partial(_mix_ffn_kernel, layer=layer),
        grid=(t // tm,),
        in_specs=[tok(D_MODEL), tok(half), tok(half),
                  _layer_spec((half, D_MODEL), 0, 0), _layer_spec((half, D_MODEL), 0, 1),
                  vec, vec, vec, hbm, hbm, hbm],
        out_specs=tok(D_MODEL),
        out_shape=jax.ShapeDtypeStruct((t, D_MODEL), F32),
        scratch_shapes=[pltpu.VMEM((D_MODEL, D_FF), BF16), pltpu.VMEM((D_MODEL, D_FF), BF16),
                        pltpu.VMEM((D_FF, D_MODEL), BF16),
                        pltpu.VMEM((2, D_MODEL // FFN_CAST_CHUNKS, D_FF), F32),
                        pltpu.VMEM((2, D_FF // FFN_CAST_CHUNKS, D_MODEL), F32),
                        pltpu.SemaphoreType.DMA((2,))],
        compiler_params=_params(("arbitrary",)),
        name="mix_out_ffn",
    )(x2d, a, b, wo, wo, g_post_mix, g_pre_ffn, g_post_ffn, w_gate, w_up, w_down)


def _l1_in_kernel(x_ref, g_ref, w_ref, wa2_ref, ba_ref, ng_ref,
                  q_ref, k_ref, v_ref, sg_ref, la_ref):
    sub = x_ref.shape[0] // IN_SUBTILES

    n_qkv = 2 * C_KEY_DIM + C_VAL_DIM
    hidden = {}

    def project(r, part):
        rows = slice(r * sub, (r + 1) * sub)
        if r not in hidden:
            hidden[r] = _rms_norm(x_ref[rows, :], g_ref[...]).astype(BF16)
        cols = slice(n_qkv, C_IN_PAD) if part == "gates" else slice(0, n_qkv)
        return jnp.dot(hidden[r], w_ref[:, cols], preferred_element_type=F32)

    def finish_qkv(r, proj):
        rows = slice(r * sub, (r + 1) * sub)
        q_ref[rows, :] = (proj[:, 0:C_KEY_DIM] * (C_DK ** -0.5)).astype(BF16)
        k_ref[rows, :] = proj[:, C_KEY_DIM:2 * C_KEY_DIM].astype(BF16)
        v_ref[rows, :] = proj[:, 2 * C_KEY_DIM:n_qkv].astype(BF16)

    def finish_gates(r, proj):
        rows = slice(r * sub, (r + 1) * sub)
        a_low = proj[:, C_VAL_DIM:C_VAL_DIM + C_GATE_RANK]
        z = (jnp.dot(a_low.astype(BF16), wa2_ref[...], preferred_element_type=F32)
             + ba_ref[...])
        log_sig = jnp.minimum(z, 0.0) - jnp.log1p(jnp.exp(-jnp.abs(z)))
        la_ref[rows, :] = log_sig * (1.0 / C_GATE_TAU)
        gg = proj[:, 0:C_VAL_DIM]
        sg_ref[rows, :] = (gg * jax.nn.sigmoid(gg) * ng_ref[...]).astype(BF16)

    items = [(r, part) for r in range(IN_SUBTILES) for part in ("gates", "qkv")]
    proj = project(*items[0])
    for n, (r, part) in enumerate(items):
        upcoming = project(*items[n + 1]) if n + 1 < len(items) else None
        (finish_gates if part == "gates" else finish_qkv)(r, proj)
        proj = upcoming


def _l1_in(x2d, g, w_in, w_a2, b_a, norm_g):
    t = x2d.shape[0]
    tm = TM_IN1
    tok = lambda n: pl.BlockSpec((tm, n), lambda i: (i, 0))
    return pl.pallas_call(
        _l1_in_kernel,
        grid=(t // tm,),
        in_specs=[tok(D_MODEL), _const_spec((1, D_MODEL)), _const_spec((D_MODEL, C_IN_PAD)),
                  _const_spec((C_GATE_RANK, C_KEY_DIM)), _const_spec((1, C_KEY_DIM)),
                  _const_spec((1, C_VAL_DIM))],
        out_specs=[tok(C_KEY_DIM), tok(C_KEY_DIM), tok(C_VAL_DIM), tok(C_VAL_DIM),
                   tok(C_KEY_DIM)],
        out_shape=[jax.ShapeDtypeStruct((t, C_KEY_DIM), BF16),
                   jax.ShapeDtypeStruct((t, C_KEY_DIM), BF16),
                   jax.ShapeDtypeStruct((t, C_VAL_DIM), BF16),
                   jax.ShapeDtypeStruct((t, C_VAL_DIM), BF16),
                   jax.ShapeDtypeStruct((t, C_KEY_DIM), F32)],
        compiler_params=_params(("parallel",)),
        name="l1_in_proj_gate",
    )(x2d, g, w_in, w_a2, b_a, norm_g)


def _gla_levels(block):
    levels = []
    s = block // 2
    while s >= 1:
        levels.append(s)
        s //= 2
    return tuple(levels)


GLA_MXU_LEVELS = (4, 2)


def _range_sum_matrix(block):
    row = lax.broadcasted_iota(jnp.int32, (block, block), 0)
    col = lax.broadcasted_iota(jnp.int32, (block, block), 1)
    mats = [(col <= row).astype(F32)]
    for s in GLA_MXU_LEVELS:
        bnd = (row & (-2 * s)) + (s - 1)
        upper = (row & s) != 0
        first = jnp.where(upper, bnd, row)
        last = jnp.where(upper, row, bnd)
        mats.append(((col > first) & (col <= last)).astype(F32))
    return jnp.concatenate(mats, axis=0).astype(BF16)


def _level_factors(c_ref, c2, small_e, q, k, r0, ks, block, s):
    if s >= 8:
        parts = []
        for p in range(block // (2 * s)):
            lo = slice(2 * s * p, 2 * s * p + s)
            up = slice(2 * s * p + s, 2 * s * (p + 1))
            b = r0 + 2 * s * p + s - 1
            ref_row = c_ref[b:b + 1, ks]
            parts.append(k[lo] * jnp.exp2(ref_row - c2[lo]))
            parts.append(q[up] * jnp.exp2(c2[up] - ref_row))
        return jnp.concatenate(parts, axis=0)
    rowv = lax.broadcasted_iota(jnp.int32, (block, 1), 0)
    upper = (rowv & s) != 0
    return jnp.where(upper, q, k) * jnp.exp2(small_e[s])


def _gla_kernel(q_ref, k_ref, v_ref, sg_ref, la_ref, ya_ref, yb_ref,
                st_ref, c_ref, m_ref):
    blk = GLA_BLOCK

    @pl.when(pl.program_id(1) == 0)
    def _reset():
        st_ref[...] = jnp.zeros_like(st_ref)
        m_ref[...] = _range_sum_matrix(blk)

    levels = _gla_levels(blk)
    row = lax.broadcasted_iota(jnp.int32, (blk, blk), 0)
    col = lax.broadcasted_iota(jnp.int32, (blk, blk), 1)
    eye = col == row
    split = jnp.where(row > col, row ^ col, 0)
    odd = (lax.broadcasted_iota(jnp.int32, (blk, 1), 0) & 1) != 0
    nt = (((1,), (1,)), ((), ()))
    tn = (((0,), (0,)), ((), ()))
    n_blocks = q_ref.shape[0] // blk

    def range_sums(ci):
        rows = slice(ci * blk, (ci + 1) * blk)
        la2 = la_ref[rows, :] * LOG2E
        hi = la2.astype(BF16)
        mid = (la2 - hi.astype(F32)).astype(BF16)
        sums = (jnp.dot(m_ref[...], hi, preferred_element_type=F32)
                + jnp.dot(m_ref[...], mid, preferred_element_type=F32))
        c_ref[rows, :] = sums[0:blk]
        return la2, sums

    def factors(ci, h, la2, sums):
        r0 = ci * blk
        rows = slice(r0, r0 + blk)
        ks = slice(h * C_DK, (h + 1) * C_DK)
        q = q_ref[rows, ks].astype(F32)
        k = k_ref[rows, ks].astype(F32)
        c2 = sums[0:blk, ks]
        small_e = {1: jnp.where(odd, la2[:, ks], 0.0)}
        for n, s in enumerate(GLA_MXU_LEVELS):
            small_e[s] = sums[(n + 1) * blk:(n + 2) * blk, ks]
        zs = [_level_factors(c_ref, c2, small_e, q, k, r0, ks, blk, s).astype(BF16)
              for s in levels]
        c_last = c_ref[r0 + blk - 1:r0 + blk, ks]
        qe = (q * jnp.exp2(c2)).astype(BF16)
        kd = (k * jnp.exp2(c_last - c2)).astype(BF16)
        diag = jnp.sum(q * k, axis=-1, keepdims=True)
        return zs, qe, kd, diag, jnp.exp2(c_last)

    def intra_scores(zs, qe, kd, diag, state_decay):
        a = jnp.where(eye, diag, 0.0)
        for li in reversed(range(len(levels))):
            gram = lax.dot_general(zs[li], zs[li], nt, preferred_element_type=F32)
            a = jnp.where(split >= levels[li], gram, a)
        return a.astype(BF16), qe, kd, state_decay

    def mix(ci, h, a, qe, kd, state_decay):
        rows = slice(ci * blk, (ci + 1) * blk)
        vs = slice(h * C_DV, (h + 1) * C_DV)
        vh = v_ref[rows, vs]
        st = st_ref[h]
        o = (jnp.dot(a, vh, preferred_element_type=F32)
             + lax.dot_general(qe, st.astype(BF16), nt, preferred_element_type=F32))
        st_ref[h] = (st * state_decay
                     + lax.dot_general(vh, kd, tn, preferred_element_type=F32))
        on = o * lax.rsqrt(jnp.mean(o * o, axis=-1, keepdims=True) + NORM_EPS)
        y = (on * sg_ref[rows, vs].astype(F32)).astype(BF16)
        if h < C_HEADS // 2:
            ya_ref[rows, h * C_DV:(h + 1) * C_DV] = y
        else:
            hh = h - C_HEADS // 2
            yb_ref[rows, hh * C_DV:(hh + 1) * C_DV] = y

    items = [(ci, h) for ci in range(n_blocks) for h in range(C_HEADS)]
    sums = {0: range_sums(0)}
    fact = {}
    scored = {}
    lag_scores, lag_mix = GLA_STAGE_LAGS
    for t in range(len(items) + lag_mix):
        if t < len(items):
            ci, h = items[t]
            if h == 0 and ci + 1 < n_blocks:
                sums[ci + 1] = range_sums(ci + 1)
            fact[t] = factors(ci, h, *sums[ci])
        if 0 <= t - lag_scores < len(items):
            scored[t - lag_scores] = intra_scores(*fact.pop(t - lag_scores))
        if 0 <= t - lag_mix < len(items):
            mix(*items[t - lag_mix], *scored.pop(t - lag_mix))


def _gla(q, k, v, sg, la, batch):
    t = q.shape[0]
    tc = TC_GLA
    nt = t // batch // tc
    tok = lambda n: pl.BlockSpec((tc, n), lambda b, i: (b * nt + i, 0))
    half = C_VAL_DIM // 2
    out = jax.ShapeDtypeStruct((t, half), BF16)
    return pl.pallas_call(
        _gla_kernel,
        grid=(batch, nt),
        in_specs=[tok(C_KEY_DIM), tok(C_KEY_DIM), tok(C_VAL_DIM), tok(C_VAL_DIM),
                  tok(C_KEY_DIM)],
        out_specs=[tok(half), tok(half)],
        out_shape=[out, out],
        scratch_shapes=[pltpu.VMEM((C_HEADS, C_DV, C_DK), F32),
                        pltpu.VMEM((tc, C_KEY_DIM), F32),
                        pltpu.VMEM(((1 + len(GLA_MXU_LEVELS)) * GLA_BLOCK, GLA_BLOCK), BF16)],
        compiler_params=_params(("parallel", "arbitrary")),
        name="l1_gla_scan",
    )(q, k, v, sg, la)


def _rel_bias_row(rel_bias):
    h = rel_bias.shape[0]
    last = rel_bias[:, 2 * A_MAX_REL:]
    head = jnp.broadcast_to(last, (h, A_MAX_REL))
    mid = rel_bias[:, ::-1]
    tail = jnp.broadcast_to(last, (h, BIAS_ROW - 3 * A_MAX_REL - 1))
    return jnp.concatenate([head, mid, tail], axis=1)


def kernel(x, pre_mix_g, post_mix_g, pre_ffn_g, post_ffn_g, ab_w_in, a_rel_bias, b_ln_g,
           b_ln_b, b_w_s, b_b_s, ab_w_out, c_w_in, c_w_a2, c_b_a, c_norm_g, c_w_out,
           ffn_w_gate, ffn_w_up, ffn_w_down):
    batch, seq, d = x.shape
    x2d = x.reshape(batch * seq, d)
    vec = lambda a: a.reshape(1, -1)
    bf = lambda a: a.astype(BF16)
    depth = pre_mix_g.shape[0]
    gains = [g.reshape(depth, 1, d) for g in (post_mix_g, pre_ffn_g, post_ffn_g)]
    ffn_w = [ffn_w_gate, ffn_w_up, ffn_w_down]

    q, k, v, b_out = _l0_in(x2d, vec(pre_mix_g[0]), bf(ab_w_in), vec(b_ln_g[0]),
                            vec(b_ln_b[0]), b_w_s[0], b_b_s[0].T)
    a_out = _attention(q, k, v, _rel_bias_row(a_rel_bias[0]), batch)
    x2d = _mix_ffn(x2d, a_out, b_out, bf(ab_w_out), 0, *gains, *ffn_w)

    w_in1 = bf(jnp.pad(c_w_in[0], ((0, 0), (0, C_IN_PAD - C_IN))))
    q, k, v, sg, la = _l1_in(x2d, vec(pre_mix_g[1]), w_in1, bf(c_w_a2[0]), vec(c_b_a[0]),
                             vec(c_norm_g[0]))
    ya, yb = _gla(q, k, v, sg, la, batch)
    x2d = _mix_ffn(x2d, ya, yb, bf(c_w_out), 1, *gains, *ffn_w)
    return x2d.reshape(batch, seq, d)
```

```python
import jax
import jax.numpy as jnp
from jax import lax
from jax.experimental import pallas as pl
from jax.experimental.pallas import tpu as pltpu

F32 = jnp.float32
BF16 = jnp.bfloat16

D_MODEL = 1024
CHUNK = 64
NORM_EPS = 1e-6
A_HEADS = 8
A_HEAD_DIM = 64
A_WIDTH = A_HEADS * A_HEAD_DIM
A_LEFT_CHUNKS = 8
A_MAX_REL = 256
B_GROUPS = 4
B_BLOCK = 128
B_WIDTH = D_MODEL // 2
AB_IN = 3 * A_WIDTH + 2 * B_WIDTH
C_HEADS = 4
C_KEY_DIM = D_MODEL // 2
C_VAL_DIM = D_MODEL
C_DK = C_KEY_DIM // C_HEADS
C_DV = C_VAL_DIM // C_HEADS
C_GATE_RANK = 16
C_GATE_TAU = 16.0
C_MAIN = 2 * C_KEY_DIM + 2 * C_VAL_DIM
C_IN = C_MAIN + C_GATE_RANK
D_FF = 2816

LANES = 128
SUBLANES = 8
C_IN_PAD = -(-C_IN // LANES) * LANES
VMEM_LIMIT_BYTES = 56 * 1024 * 1024

NEG_BIG = float(jnp.finfo(jnp.float32).min)
LOG2E = 1.4426950408889634

TM_IN0 = 1024
IN_SUBTILES = 4
TQ_ATTN = 256
ATTN_TILES = 2
TM_FFN = 1024
FFN_SUBTILES = 4
TM_IN1 = 1024
TC_GLA = 1024
GLA_BLOCK = 128
GLA_STAGE_LAGS = (1, 2)
ATTN_KEYS = TQ_ATTN + A_LEFT_CHUNKS * CHUNK
BIAS_ROW = 1024


def _rms_norm(x, g):
    ms = jnp.mean(x * x, axis=-1, keepdims=True)
    return x * lax.rsqrt(ms + NORM_EPS) * g


def _gelu(x):
    return 0.5 * x * (1.0 + lax.erf(x * (2.0 ** -0.5)))


def _const_spec(shape):
    nd = len(shape)
    return pl.BlockSpec(shape, lambda *_: (0,) * nd, pipeline_mode=pl.Buffered(1))


def _layer_spec(shape, layer, row_block=0):
    return pl.BlockSpec((None,) + tuple(shape), lambda *_: (layer, row_block, 0),
                        pipeline_mode=pl.Buffered(1))


def _params(sem):
    return pltpu.CompilerParams(dimension_semantics=sem,
                                vmem_limit_bytes=VMEM_LIMIT_BYTES)


def _l0_in_kernel(x_ref, g_ref, w_ref, lng_ref, lnb_ref, ws_ref, bst_ref,
                  q_ref, k_ref, v_ref, b_ref):
    sub = x_ref.shape[0] // IN_SUBTILES
    row = lax.broadcasted_iota(jnp.int32, (B_BLOCK, B_BLOCK), 0)
    col = lax.broadcasted_iota(jnp.int32, (B_BLOCK, B_BLOCK), 1)
    causal = col <= row
    w_gate = [jnp.where(causal, ws_ref[g], 0.0).astype(BF16) for g in range(B_GROUPS)]

    n_attn = 3 * A_WIDTH
    hidden = {}

    def project(r, part):
        rows = slice(r * sub, (r + 1) * sub)
        if r not in hidden:
            hidden[r] = _rms_norm(x_ref[rows, :], g_ref[...]).astype(BF16)
        cols = slice(n_attn, AB_IN) if part == "gmlp" else slice(0, n_attn)
        return jnp.dot(hidden[r], w_ref[:, cols], preferred_element_type=F32)

    def finish_attn(r, proj):
        rows = slice(r * sub, (r + 1) * sub)
        q_ref[rows, :] = (proj[:, 0:A_WIDTH] * (A_HEAD_DIM ** -0.5 * LOG2E)).astype(BF16)
        k_ref[rows, :] = proj[:, A_WIDTH:2 * A_WIDTH].astype(BF16)
        v_ref[rows, :] = proj[:, 2 * A_WIDTH:3 * A_WIDTH].astype(BF16)

    def finish_gmlp(r, proj):
        u = _gelu(proj[:, 0:B_WIDTH])
        zv = _gelu(proj[:, B_WIDTH:2 * B_WIDTH])
        mu = jnp.mean(zv, axis=-1, keepdims=True)
        d = zv - mu
        var = jnp.mean(d * d, axis=-1, keepdims=True)
        vln = (d * lax.rsqrt(var + NORM_EPS) * lng_ref[...] + lnb_ref[...]).astype(BF16)
        for g in range(B_GROUPS):
            bcol = bst_ref[:, g:g + 1]
            cs = slice(g * LANES, (g + 1) * LANES)
            for n in range(sub // B_BLOCK):
                rs = slice(n * B_BLOCK, (n + 1) * B_BLOCK)
                f = jnp.dot(w_gate[g], vln[rs, cs], preferred_element_type=F32) + bcol
                out_rows = slice(r * sub + n * B_BLOCK, r * sub + (n + 1) * B_BLOCK)
                b_ref[out_rows, cs] = (u[rs, cs] * f).astype(BF16)

    items = [(r, part) for r in range(IN_SUBTILES) for part in ("gmlp", "attn")]
    proj = project(*items[0])
    for n, (r, part) in enumerate(items):
        upcoming = project(*items[n + 1]) if n + 1 < len(items) else None
        (finish_gmlp if part == "gmlp" else finish_attn)(r, proj)
        proj = upcoming


def _l0_in(x2d, g, w_in, ln_g, ln_b, w_s, b_s_t):
    t = x2d.shape[0]
    tm = TM_IN0
    tok = lambda n: pl.BlockSpec((tm, n), lambda i: (i, 0))
    out = jax.ShapeDtypeStruct((t, A_WIDTH), BF16)
    return pl.pallas_call(
        _l0_in_kernel,
        grid=(t // tm,),
        in_specs=[tok(D_MODEL), _const_spec((1, D_MODEL)),
                  _layer_spec((D_MODEL, AB_IN), 0),
                  _const_spec((1, B_WIDTH)), _const_spec((1, B_WIDTH)),
                  _const_spec((B_GROUPS, B_BLOCK, B_BLOCK)), _const_spec((B_BLOCK, B_GROUPS))],
        out_specs=[tok(A_WIDTH)] * 4,
        out_shape=[out] * 4,
        compiler_params=_params(("parallel",)),
        name="l0_in_proj_gmlp",
    )(x2d, g, w_in, ln_g, ln_b, w_s, b_s_t)


def _attn_kernel(q_ref, kp_ref, kc_ref, vp_ref, vc_ref, rb_ref, o_ref, bias_ref):
    i = pl.program_id(1)
    tq = TQ_ATTN
    nk = ATTN_KEYS
    step = ATTN_TILES * tq
    left = A_LEFT_CHUNKS * CHUNK

    @pl.when((i - 1) * step < left)
    def _build_bias():
        qq = lax.broadcasted_iota(jnp.int32, (tq, nk), 0)
        kk = lax.broadcasted_iota(jnp.int32, (tq, nk), 1)
        band_lo = (qq // CHUNK) * CHUNK
        in_band = (kk >= band_lo) & (kk < band_lo + (A_LEFT_CHUNKS + 1) * CHUNK)
        for h in range(A_HEADS):
            r = jnp.broadcast_to(rb_ref[h:h + 1, :], (tq, BIAS_ROW))
            t = pltpu.roll(r, 0, 1, stride=1, stride_axis=0)[:, :nk] * LOG2E
            for tile in range(ATTN_TILES):
                first_key = i * step + tile * tq - left
                bias_ref[tile, h] = jnp.where(in_band & (kk + first_key >= 0), t, NEG_BIG)

    kcat = jnp.concatenate([kp_ref[...], kc_ref[...]], axis=0)
    vcat = jnp.concatenate([vp_ref[...], vc_ref[...]], axis=0)
    lane = lax.broadcasted_iota(jnp.int32, (1, LANES), 1)
    lo = lane < A_HEAD_DIM
    nt = (((1,), (1,)), ((), ()))

    def head_sel(h):
        return lo if h % 2 == 0 else jnp.logical_not(lo)

    def scores(tile, h):
        cs = slice((h // 2) * LANES, (h // 2 + 1) * LANES)
        qp = q_ref[tile * tq:(tile + 1) * tq, cs]
        qh = jnp.where(head_sel(h), qp, jnp.zeros_like(qp))
        return lax.dot_general(qh, kcat[tile * tq:tile * tq + nk, cs], nt,
                               preferred_element_type=F32)

    def biased_scores(tile, h, s):
        s = s + bias_ref[tile, h]
        return s, jnp.max(s, axis=-1, keepdims=True)

    def softmax_numerator(s, m):
        return jnp.exp2(s - m).astype(BF16)

    def weighted_values(tile, h, e):
        cs = slice((h // 2) * LANES, (h // 2 + 1) * LANES)
        ones_lane = A_HEAD_DIM if h % 2 == 0 else 0
        vh = jnp.where(head_sel(h), vcat[tile * tq:tile * tq + nk, cs],
                       (lane == ones_lane).astype(BF16))
        o = jnp.dot(e, vh, preferred_element_type=F32)
        return o / o[:, ones_lane:ones_lane + 1]

    items = [(tile, h) for tile in range(ATTN_TILES) for h in range(A_HEADS)]
    s_val = {}
    b_val = {}
    e_val = {}
    outs = {}
    for t in range(len(items) + 3):
        if t < len(items):
            s_val[t] = scores(*items[t])
        if 0 <= t - 1 < len(items):
            b_val[t - 1] = biased_scores(*items[t - 1], s_val.pop(t - 1))
        if 0 <= t - 2 < len(items):
            e_val[t - 2] = softmax_numerator(*b_val.pop(t - 2))
        n = t - 3
        if 0 <= n < len(items):
            tile, h = items[n]
            outs[h] = weighted_values(tile, h, e_val.pop(n))
            if h % 2 == 1:
                cs = slice((h // 2) * LANES, (h // 2 + 1) * LANES)
                o_ref[tile * tq:(tile + 1) * tq, cs] = jnp.where(
                    lo, outs.pop(h - 1), outs.pop(h)).astype(BF16)


def _attention(q, k, v, rb_row, batch):
    t = q.shape[0]
    step = ATTN_TILES * TQ_ATTN
    assert step == A_LEFT_CHUNKS * CHUNK
    nt = t // batch // step
    cur = pl.BlockSpec((step, A_WIDTH), lambda b, i: (b * nt + i, 0))
    prev = pl.BlockSpec((step, A_WIDTH), lambda b, i: (b * nt + jnp.maximum(i - 1, 0), 0))
    return pl.pallas_call(
        _attn_kernel,
        grid=(batch, nt),
        in_specs=[cur, prev, cur, prev, cur, _const_spec((A_HEADS, BIAS_ROW))],
        out_specs=cur,
        out_shape=jax.ShapeDtypeStruct((t, A_WIDTH), BF16),
        scratch_shapes=[pltpu.VMEM((ATTN_TILES, A_HEADS, TQ_ATTN, ATTN_KEYS), F32)],
        compiler_params=_params(("parallel", "arbitrary")),
        name="l0_band_attention",
    )(q, k, k, v, v, rb_row)


def _mix_ffn_kernel(x_ref, a_ref, b_ref, woa_ref, wob_ref, gpm_ref, gpf_ref, gpo_ref,
                    wg_ref, wu_ref, wd_ref, o_ref):
    sub = x_ref.shape[0] // FFN_SUBTILES
    tiles = [slice(r * sub, (r + 1) * sub) for r in range(FFN_SUBTILES)]

    def mix_proj(rows, _):
        return (jnp.dot(a_ref[rows, :], woa_ref[...], preferred_element_type=F32)
                + jnp.dot(b_ref[rows, :], wob_ref[...], preferred_element_type=F32))

    def norms(rows, m):
        x1 = x_ref[rows, :] + _rms_norm(m, gpm_ref[...])
        return x1, _rms_norm(x1, gpf_ref[...]).astype(BF16)

    def gate_up(rows, st):
        x1, h = st
        return (x1, jnp.dot(h, wg_ref[...], preferred_element_type=F32),
                jnp.dot(h, wu_ref[...], preferred_element_type=F32))

    def activation(rows, st):
        x1, gate, up = st
        return x1, (gate * jax.nn.sigmoid(gate) * up).astype(BF16)

    def down(rows, st):
        x1, act = st
        return x1, jnp.dot(act, wd_ref[...], preferred_element_type=F32)

    def finish(rows, st):
        x1, y = st
        o_ref[rows, :] = x1 + _rms_norm(y, gpo_ref[...])
        return None

    stages = [mix_proj, norms, gate_up, activation, down, finish]
    state = [None] * FFN_SUBTILES
    for step in range(len(stages) + FFN_SUBTILES - 1):
        for r in range(FFN_SUBTILES):
            t = step - r
            if 0 <= t < len(stages):
                state[r] = stages[t](tiles[r], state[r])


def _mix_ffn(x2d, a, b, wo, layer, g_post_mix, g_pre_ffn, g_post_ffn, w_gate, w_up, w_down):
    t = x2d.shape[0]
    tm = TM_FFN
    half = wo.shape[1] // 2
    tok = lambda n: pl.BlockSpec((tm, n), lambda i: (i, 0))
    vec = _layer_spec((1, D_MODEL), layer)
    return pl.pallas_call(
        _mix_ffn_kernel,
        grid=(t // tm,),
        in_specs=[tok(D_MODEL), tok(half), tok(half),
                  _layer_spec((half, D_MODEL), 0, 0), _layer_spec((half, D_MODEL), 0, 1),
                  vec, vec, vec,
                  _layer_spec((D_MODEL, D_FF), layer), _layer_spec((D_MODEL, D_FF), layer),
                  _layer_spec((D_FF, D_MODEL), layer)],
        out_specs=tok(D_MODEL),
        out_shape=jax.ShapeDtypeStruct((t, D_MODEL), F32),
        compiler_params=_params(("parallel",)),
        name="mix_out_ffn",
    )(x2d, a, b, wo, wo, g_post_mix, g_pre_ffn, g_post_ffn, w_gate, w_up, w_down)


def _l1_in_kernel(x_ref, g_ref, w_ref, wa2_ref, ba_ref,
                  q_ref, k_ref, v_ref, sg_ref, la_ref):
    sub = x_ref.shape[0] // IN_SUBTILES

    n_qkv = 2 * C_KEY_DIM + C_VAL_DIM
    hidden = {}

    def project(r, part):
        rows = slice(r * sub, (r + 1) * sub)
        if r not in hidden:
            hidden[r] = _rms_norm(x_ref[rows, :], g_ref[...]).astype(BF16)
        cols = slice(n_qkv, C_IN_PAD) if part == "gates" else slice(0, n_qkv)
        return jnp.dot(hidden[r], w_ref[:, cols], preferred_element_type=F32)

    def finish_qkv(r, proj):
        rows = slice(r * sub, (r + 1) * sub)
        q_ref[rows, :] = (proj[:, 0:C_KEY_DIM] * (C_DK ** -0.5)).astype(BF16)
        k_ref[rows, :] = proj[:, C_KEY_DIM:2 * C_KEY_DIM].astype(BF16)
        v_ref[rows, :] = proj[:, 2 * C_KEY_DIM:n_qkv].astype(BF16)

    def finish_gates(r, proj):
        rows = slice(r * sub, (r + 1) * sub)
        a_low = proj[:, C_VAL_DIM:C_VAL_DIM + C_GATE_RANK]
        z = (jnp.dot(a_low.astype(BF16), wa2_ref[...], preferred_element_type=F32)
             + ba_ref[...])
        log_sig = jnp.minimum(z, 0.0) - jnp.log1p(jnp.exp(-jnp.abs(z)))
        la_ref[rows, :] = log_sig * (1.0 / C_GATE_TAU)
        gg = proj[:, 0:C_VAL_DIM]
        sg_ref[rows, :] = (gg * jax.nn.sigmoid(gg)).astype(BF16)

    items = [(r, part) for r in range(IN_SUBTILES) for part in ("gates", "qkv")]
    proj = project(*items[0])
    for n, (r, part) in enumerate(items):
        upcoming = project(*items[n + 1]) if n + 1 < len(items) else None
        (finish_gates if part == "gates" else finish_qkv)(r, proj)
        proj = upcoming


def _l1_in(x2d, g, w_in, w_a2, b_a):
    t = x2d.shape[0]
    tm = TM_IN1
    tok = lambda n: pl.BlockSpec((tm, n), lambda i: (i, 0))
    return pl.pallas_call(
        _l1_in_kernel,
        grid=(t // tm,),
        in_specs=[tok(D_MODEL), _const_spec((1, D_MODEL)), _const_spec((D_MODEL, C_IN_PAD)),
                  _const_spec((C_GATE_RANK, C_KEY_DIM)), _const_spec((1, C_KEY_DIM))],
        out_specs=[tok(C_KEY_DIM), tok(C_KEY_DIM), tok(C_VAL_DIM), tok(C_VAL_DIM),
                   tok(C_KEY_DIM)],
        out_shape=[jax.ShapeDtypeStruct((t, C_KEY_DIM), BF16),
                   jax.ShapeDtypeStruct((t, C_KEY_DIM), BF16),
                   jax.ShapeDtypeStruct((t, C_VAL_DIM), BF16),
                   jax.ShapeDtypeStruct((t, C_VAL_DIM), BF16),
                   jax.ShapeDtypeStruct((t, C_KEY_DIM), F32)],
        compiler_params=_params(("parallel",)),
        name="l1_in_proj_gate",
    )(x2d, g, w_in, w_a2, b_a)


def _gla_levels(block):
    levels = []
    s = block // 2
    while s >= 1:
        levels.append(s)
        s //= 2
    return tuple(levels)


GLA_MXU_LEVELS = (4, 2)


def _range_sum_matrix(block):
    row = lax.broadcasted_iota(jnp.int32, (block, block), 0)
    col = lax.broadcasted_iota(jnp.int32, (block, block), 1)
    mats = [(col <= row).astype(F32)]
    for s in GLA_MXU_LEVELS:
        bnd = (row & (-2 * s)) + (s - 1)
        upper = (row & s) != 0
        first = jnp.where(upper, bnd, row)
        last = jnp.where(upper, row, bnd)
        mats.append(((col > first) & (col <= last)).astype(F32))
    return jnp.concatenate(mats, axis=0).astype(BF16)


def _level_factors(c_ref, c2, small_e, q, k, r0, ks, block, s):
    if s >= SUBLANES:
        parts = []
        for p in range(block // (2 * s)):
            lo = slice(2 * s * p, 2 * s * p + s)
            up = slice(2 * s * p + s, 2 * s * (p + 1))
            b = r0 + 2 * s * p + s - 1
            ref_row = c_ref[b:b + 1, ks]
            parts.append(k[lo] * jnp.exp2(ref_row - c2[lo]))
            parts.append(q[up] * jnp.exp2(c2[up] - ref_row))
        return jnp.concatenate(parts, axis=0)
    rowv = lax.broadcasted_iota(jnp.int32, (block, 1), 0)
    upper = (rowv & s) != 0
    return jnp.where(upper, q, k) * jnp.exp2(small_e[s])


def _gla_kernel(q_ref, k_ref, v_ref, sg_ref, la_ref, ng_ref, ya_ref, yb_ref,
                st_ref, c_ref, m_ref):
    blk = GLA_BLOCK

    @pl.when(pl.program_id(1) == 0)
    def _reset():
        st_ref[...] = jnp.zeros_like(st_ref)
        m_ref[...] = _range_sum_matrix(blk)

    levels = _gla_levels(blk)
    row = lax.broadcasted_iota(jnp.int32, (blk, blk), 0)
    col = lax.broadcasted_iota(jnp.int32, (blk, blk), 1)
    eye = col == row
    split = jnp.where(row > col, row ^ col, 0)
    odd = (lax.broadcasted_iota(jnp.int32, (blk, 1), 0) & 1) != 0
    nt = (((1,), (1,)), ((), ()))
    tn = (((0,), (0,)), ((), ()))
    n_blocks = q_ref.shape[0] // blk

    def range_sums(ci):
        rows = slice(ci * blk, (ci + 1) * blk)
        la2 = la_ref[rows, :] * LOG2E
        hi = la2.astype(BF16)
        mid = (la2 - hi.astype(F32)).astype(BF16)
        sums = (jnp.dot(m_ref[...], hi, preferred_element_type=F32)
                + jnp.dot(m_ref[...], mid, preferred_element_type=F32))
        c_ref[rows, :] = sums[0:blk]
        return la2, sums

    def factors(ci, h, la2, sums):
        r0 = ci * blk
        rows = slice(r0, r0 + blk)
        ks = slice(h * C_DK, (h + 1) * C_DK)
        q = q_ref[rows, ks].astype(F32)
        k = k_ref[rows, ks].astype(F32)
        c2 = sums[0:blk, ks]
        small_e = {1: jnp.where(odd, la2[:, ks], 0.0)}
        for n, s in enumerate(GLA_MXU_LEVELS):
            small_e[s] = sums[(n + 1) * blk:(n + 2) * blk, ks]
        zs = [_level_factors(c_ref, c2, small_e, q, k, r0, ks, blk, s).astype(BF16)
              for s in levels]
        c_last = c_ref[r0 + blk - 1:r0 + blk, ks]
        qe = (q * jnp.exp2(c2)).astype(BF16)
        kd = (k * jnp.exp2(c_last - c2)).astype(BF16)
        diag = jnp.sum(q * k, axis=-1, keepdims=True)
        return zs, qe, kd, diag, jnp.exp2(c_last)

    def intra_scores(zs, qe, kd, diag, state_decay):
        a = jnp.where(eye, diag, 0.0)
        for li in reversed(range(len(levels))):
            gram = lax.dot_general(zs[li], zs[li], nt, preferred_element_type=F32)
            a = jnp.where(split >= levels[li], gram, a)
        return a.astype(BF16), qe, kd, state_decay

    def mix(ci, h, a, qe, kd, state_decay):
        rows = slice(ci * blk, (ci + 1) * blk)
        vs = slice(h * C_DV, (h + 1) * C_DV)
        vh = v_ref[rows, vs]
        st = st_ref[h]
        o = (jnp.dot(a, vh, preferred_element_type=F32)
             + lax.dot_general(qe, st.astype(BF16), nt, preferred_element_type=F32))
        st_ref[h] = (st * state_decay
                     + lax.dot_general(vh, kd, tn, preferred_element_type=F32))
        on = o * lax.rsqrt(jnp.mean(o * o, axis=-1, keepdims=True) + NORM_EPS)
        y = (on * ng_ref[:, vs] * sg_ref[rows, vs].astype(F32)).astype(BF16)
        if h < C_HEADS // 2:
            ya_ref[rows, h * C_DV:(h + 1) * C_DV] = y
        else:
            hh = h - C_HEADS // 2
            yb_ref[rows, hh * C_DV:(hh + 1) * C_DV] = y

    items = [(ci, h) for ci in range(n_blocks) for h in range(C_HEADS)]
    sums = {0: range_sums(0)}
    fact = {}
    scored = {}
    lag_scores, lag_mix = GLA_STAGE_LAGS
    for t in range(len(items) + lag_mix):
        if t < len(items):
            ci, h = items[t]
            if h == 0 and ci + 1 < n_blocks:
                sums[ci + 1] = range_sums(ci + 1)
            fact[t] = factors(ci, h, *sums[ci])
        if 0 <= t - lag_scores < len(items):
            scored[t - lag_scores] = intra_scores(*fact.pop(t - lag_scores))
        if 0 <= t - lag_mix < len(items):
            mix(*items[t - lag_mix], *scored.pop(t - lag_mix))


def _gla(q, k, v, sg, la, norm_g, batch):
    t = q.shape[0]
    tc = TC_GLA
    nt = t // batch // tc
    tok = lambda n: pl.BlockSpec((tc, n), lambda b, i: (b * nt + i, 0))
    half = C_VAL_DIM // 2
    out = jax.ShapeDtypeStruct((t, half), BF16)
    return pl.pallas_call(
        _gla_kernel,
        grid=(batch, nt),
        in_specs=[tok(C_KEY_DIM), tok(C_KEY_DIM), tok(C_VAL_DIM), tok(C_VAL_DIM),
                  tok(C_KEY_DIM), _const_spec((1, C_VAL_DIM))],
        out_specs=[tok(half), tok(half)],
        out_shape=[out, out],
        scratch_shapes=[pltpu.VMEM((C_HEADS, C_DV, C_DK), F32),
                        pltpu.VMEM((tc, C_KEY_DIM), F32),
                        pltpu.VMEM(((1 + len(GLA_MXU_LEVELS)) * GLA_BLOCK, GLA_BLOCK), BF16)],
        compiler_params=_params(("parallel", "arbitrary")),
        name="l1_gla_scan",
    )(q, k, v, sg, la, norm_g)


def _rel_bias_row(rel_bias):
    h = rel_bias.shape[0]
    last = rel_bias[:, 2 * A_MAX_REL:]
    head = jnp.broadcast_to(last, (h, A_MAX_REL))
    mid = rel_bias[:, ::-1]
    tail = jnp.broadcast_to(last, (h, BIAS_ROW - 3 * A_MAX_REL - 1))
    return jnp.concatenate([head, mid, tail], axis=1)


def kernel(x, pre_mix_g, post_mix_g, pre_ffn_g, post_ffn_g, ab_w_in, a_rel_bias, b_ln_g,
           b_ln_b, b_w_s, b_b_s, ab_w_out, c_w_in, c_w_a2, c_b_a, c_norm_g, c_w_out,
           ffn_w_gate, ffn_w_up, ffn_w_down):
    batch, seq, d = x.shape
    x2d = x.reshape(batch * seq, d)
    vec = lambda a: a.reshape(1, -1)
    bf = lambda a: a.astype(BF16)
    depth = pre_mix_g.shape[0]
    gains = [g.reshape(depth, 1, d) for g in (post_mix_g, pre_ffn_g, post_ffn_g)]
    ffn_w = [bf(ffn_w_gate), bf(ffn_w_up), bf(ffn_w_down)]

    q, k, v, b_out = _l0_in(x2d, vec(pre_mix_g[0]), bf(ab_w_in), vec(b_ln_g[0]),
                            vec(b_ln_b[0]), b_w_s[0], b_b_s[0].T)
    a_out = _attention(q, k, v, _rel_bias_row(a_rel_bias[0]), batch)
    x2d = _mix_ffn(x2d, a_out, b_out, bf(ab_w_out), 0, *gains, *ffn_w)

    w_in1 = bf(jnp.pad(c_w_in[0], ((0, 0), (0, C_IN_PAD - C_IN))))
    q, k, v, sg, la = _l1_in(x2d, vec(pre_mix_g[1]), w_in1, bf(c_w_a2[0]), vec(c_b_a[0]))
    ya, yb = _gla(q, k, v, sg, la, vec(c_norm_g[0]), batch)
    x2d = _mix_ffn(x2d, ya, yb, bf(c_w_out), 1, *gains, *ffn_w)
    return x2d.reshape(batch, seq, d)
```
